```python
import math
import jax, jax.numpy as jnp
from jax import lax
import numpy as np

D_MODEL = 1024
BATCH = 8
SEQ = 2048
DEPTH = 1

CHUNK = 64
Q_BLOCK = 128
NORM_EPS = 1e-5
POOL_WIDTH = D_MODEL // 2
POOL_WINDOWS = (2, 4, 8, 16)
POOL_GROUPS = len(POOL_WINDOWS)
POOL_GROUP_DIM = POOL_WIDTH // POOL_GROUPS
DIFF_HEADS = 4
DIFF_HEAD_DIM = 64
DIFF_V_DIM = 2 * DIFF_HEAD_DIM
DIFF_WIDTH = DIFF_HEADS * DIFF_V_DIM
QK_WIDTH = DIFF_HEADS * 2 * DIFF_HEAD_DIM
MIX_WIDTH = POOL_WIDTH + DIFF_WIDTH
IN_WIDTH = POOL_WIDTH + 2 * QK_WIDTH + DIFF_WIDTH
ROT_DIM = DIFF_HEAD_DIM // 4
ROPE_THETA = 500000.0
MEM_LEN = 256
X_HEADS = 4
X_HEAD_DIM = D_MODEL // X_HEADS
N_EXPERTS = 32
TOP_K = 4
D_EXPERT = D_MODEL
SWIGLU_ALPHA = 1.702
SWIGLU_LIMIT = 7.0
EXPERT_BLOCK = 128

kernel_name = 'hybrid_pool_diffattn_moe_block'


def rms_norm(x, g):
    xf = x.astype(jnp.float32)
    y = xf * lax.rsqrt(jnp.mean(xf * xf, axis=-1, keepdims=True) + NORM_EPS)
    return (y * g.astype(jnp.float32)).astype(x.dtype)


def multiscale_pool(u, w_pool, pool_scale):
    B, S, _ = u.shape
    uf = u.astype(jnp.float32).reshape(B, S, POOL_GROUPS, POOL_GROUP_DIM)
    csum = jnp.pad(jnp.cumsum(uf, axis=1), ((0, 0), (1, 0), (0, 0), (0, 0)))
    t = jnp.arange(S)[:, None]
    win = jnp.array(POOL_WINDOWS, dtype=jnp.int32)[None, :]
    lo = jnp.maximum(t + 1 - win, 0)
    gidx = jnp.arange(POOL_GROUPS)[None, :]
    window_sum = csum[:, 1:] - csum[:, lo, gidx]
    count = (t + 1 - lo).astype(jnp.float32)
    mixed = (window_sum / count[None, :, :, None] - uf).astype(u.dtype)
    y = jnp.einsum('bsgc,gcd->bsgd', mixed, w_pool)
    y = y * pool_scale.reshape(POOL_GROUPS, POOL_GROUP_DIM)
    return y.reshape(B, S, POOL_WIDTH)


def rotary_tables(positions):
    inv_freq = ROPE_THETA ** (-jnp.arange(0, ROT_DIM, 2, dtype=jnp.float32) / ROT_DIM)
    ang = positions.astype(jnp.float32)[..., None] * inv_freq
    return jnp.cos(ang)[:, :, None, None, :], jnp.sin(ang)[:, :, None, None, :]


def partial_rotary(x, cos, sin):
    half = ROT_DIM // 2
    xr = x[..., :ROT_DIM].astype(jnp.float32)
    x1, x2 = xr[..., :half], xr[..., half:]
    rot = jnp.concatenate([x1 * cos - x2 * sin, x2 * cos + x1 * sin], axis=-1)
    return jnp.concatenate([rot.astype(x.dtype), x[..., ROT_DIM:]], axis=-1)


def diff_attention(q, k, v, lam):
    S = q.shape[3]
    scale = DIFF_HEAD_DIM ** -0.5
    outs = []
    for qb in range(S // Q_BLOCK):
        q0 = qb * Q_BLOCK
        kend = q0 + Q_BLOCK
        s = jnp.einsum('bhmqd,bhmkd->bhmqk', q[:, :, :, q0:kend], k[:, :, :, :kend],
                       preferred_element_type=jnp.float32) * scale
        q_chunk = (q0 + jnp.arange(Q_BLOCK)) // CHUNK
        k_chunk = jnp.arange(kend) // CHUNK
        mask = k_chunk[None, :] <= q_chunk[:, None]
        p = jax.nn.softmax(jnp.where(mask, s, -jnp.inf), axis=-1)
        a = p[:, :, 0] - lam * p[:, :, 1]
        outs.append(jnp.einsum('bhqk,bhkv->bhqv', a.astype(v.dtype), v[:, :, :kend]))
    return jnp.concatenate(outs, axis=2)


def cross_attention(h, mem_n, w_cq, w_ckv, w_co):
    B, S, _ = h.shape
    M = mem_n.shape[1]
    q = (h @ w_cq).reshape(B, S, X_HEADS, X_HEAD_DIM)
    kv = (mem_n @ w_ckv).reshape(B, M, 2, X_HEADS, X_HEAD_DIM)
    k, v = kv[:, :, 0], kv[:, :, 1]
    s = jnp.einsum('bshd,bmhd->bhsm', q, k, preferred_element_type=jnp.float32) * (X_HEAD_DIM ** -0.5)
    p = jax.nn.softmax(s, axis=-1)
    o = jnp.einsum('bhsm,bmhd->bshd', p.astype(v.dtype), v).reshape(B, S, D_MODEL)
    return o @ w_co


def clamped_swiglu(gu):
    gate, up = gu[..., :D_EXPERT], gu[..., D_EXPERT:]
    gate = jnp.minimum(gate, SWIGLU_LIMIT)
    up = jnp.clip(up, -SWIGLU_LIMIT, SWIGLU_LIMIT)
    return (up + 1.0) * gate * jax.nn.sigmoid(SWIGLU_ALPHA * gate)


def moe_ffn(h, w_router, b_router, w_gu, b_gu, w_down, b_down):
    B, S, D = h.shape
    T = B * S
    xt = h.reshape(T, D)
    logits = (xt @ w_router + b_router).astype(jnp.float32)
    top_val, top_idx = lax.top_k(logits, TOP_K)
    gates = jax.nn.softmax(top_val, axis=-1)
    flat_e = top_idx.reshape(-1)
    flat_tok = jnp.repeat(jnp.arange(T, dtype=jnp.int32), TOP_K)
    flat_g = gates.reshape(-1)
    order = jnp.argsort(flat_e)
    se = flat_e[order]
    counts = jnp.bincount(flat_e, length=N_EXPERTS)
    start = jnp.cumsum(counts) - counts
    pcounts = (counts + EXPERT_BLOCK - 1) // EXPERT_BLOCK * EXPERT_BLOCK
    pend = jnp.cumsum(pcounts)
    pstart = pend - pcounts
    dest = pstart[se] + jnp.arange(T * TOP_K) - start[se]
    n_rows = T * TOP_K + N_EXPERTS * EXPERT_BLOCK
    n_blocks = n_rows // EXPERT_BLOCK
    row_tok = jnp.zeros((n_rows,), jnp.int32).at[dest].set(flat_tok[order])
    row_gate = jnp.zeros((n_rows,), jnp.float32).at[dest].set(flat_g[order])
    block_e = jnp.minimum(jnp.searchsorted(pend, jnp.arange(n_blocks) * EXPERT_BLOCK, side='right'),
                          N_EXPERTS - 1)

    def expert_block(args):
        tok, g, e = args
        xb = xt[tok]
        hb = clamped_swiglu(xb @ w_gu[e] + b_gu[e])
        yb = hb @ w_down[e] + b_down[e]
        return yb * g[:, None].astype(yb.dtype)

    y_rows = lax.map(expert_block, (row_tok.reshape(n_blocks, EXPERT_BLOCK),
                                    row_gate.reshape(n_blocks, EXPERT_BLOCK), block_e))
    y = jnp.zeros((T, D), h.dtype).at[row_tok].add(y_rows.reshape(n_rows, D).astype(h.dtype))
    return y.reshape(B, S, D)


def setup_inputs(seed: int = 0) -> dict:
    key = jax.random.key(seed)
    ks = jax.random.split(key, 32)
    f32 = jnp.float32
    L = DEPTH

    def nrm(k, shape, scale):
        return jax.random.normal(k, shape, f32) * scale

    def gain(k, shape):
        return 1.0 + 0.02 * jax.random.normal(k, shape, f32)

    offsets = jax.random.randint(ks[1], (BATCH, 1), 0, 4096, dtype=jnp.int32)
    return {
        'x': nrm(ks[0], (BATCH, SEQ, D_MODEL), 1.0),
        'positions': offsets + jnp.arange(SEQ, dtype=jnp.int32)[None, :],
        'mem': nrm(ks[2], (BATCH, MEM_LEN, D_MODEL), 1.0),
        'attn_norm_g': gain(ks[3], (L, D_MODEL)),
        'w_in': nrm(ks[4], (L, D_MODEL, IN_WIDTH), D_MODEL ** -0.5),
        'w_pool': nrm(ks[5], (L, POOL_GROUPS, POOL_GROUP_DIM, POOL_GROUP_DIM), POOL_GROUP_DIM ** -0.5),
        'pool_scale': gain(ks[6], (L, POOL_WIDTH)),
        'lambda_q1': nrm(ks[7], (L, DIFF_HEAD_DIM), 0.1),
        'lambda_k1': nrm(ks[8], (L, DIFF_HEAD_DIM), 0.1),
        'lambda_q2': nrm(ks[9], (L, DIFF_HEAD_DIM), 0.1),
        'lambda_k2': nrm(ks[10], (L, DIFF_HEAD_DIM), 0.1),
        'subln_g': gain(ks[11], (L, DIFF_V_DIM)),
        'w_out': nrm(ks[12], (L, MIX_WIDTH, D_MODEL), MIX_WIDTH ** -0.5),
        'xattn_norm_g': gain(ks[13], (L, D_MODEL)),
        'mem_norm_g': gain(ks[14], (L, D_MODEL)),
        'w_cq': nrm(ks[15], (L, D_MODEL, D_MODEL), D_MODEL ** -0.5),
        'w_ckv': nrm(ks[16], (L, D_MODEL, 2 * D_MODEL), D_MODEL ** -0.5),
        'w_co': nrm(ks[17], (L, D_MODEL, D_MODEL), D_MODEL ** -0.5),
        'ffn_norm_g': gain(ks[18], (L, D_MODEL)),
        'w_router': nrm(ks[19], (L, D_MODEL, N_EXPERTS), D_MODEL ** -0.5),
        'b_router': nrm(ks[20], (L, N_EXPERTS), 0.01),
        'w_gu': nrm(ks[21], (L, N_EXPERTS, D_MODEL, 2 * D_EXPERT), D_MODEL ** -0.5),
        'b_gu': nrm(ks[22], (L, N_EXPERTS, 2 * D_EXPERT), 0.01),
        'w_down': nrm(ks[23], (L, N_EXPERTS, D_EXPERT, D_MODEL), D_EXPERT ** -0.5),
        'b_down': nrm(ks[24], (L, N_EXPERTS, D_MODEL), 0.01),
        'final_norm_g': gain(ks[25], (D_MODEL,)),
    }


def reference(x, positions, mem, attn_norm_g, w_in, w_pool, pool_scale, lambda_q1, lambda_k1,
              lambda_q2, lambda_k2, subln_g, w_out, xattn_norm_g, mem_norm_g, w_cq, w_ckv, w_co,
              ffn_norm_g, w_router, b_router, w_gu, b_gu, w_down, b_down, final_norm_g):
    B, S, _ = x.shape
    cos, sin = rotary_tables(positions)
    for l in range(DEPTH):
        h = rms_norm(x, attn_norm_g[l])
        u = h @ w_in[l]
        u_pool, u_q, u_k, u_v = jnp.split(
            u, [POOL_WIDTH, POOL_WIDTH + QK_WIDTH, POOL_WIDTH + 2 * QK_WIDTH], axis=-1)
        y_pool = multiscale_pool(u_pool, w_pool[l], pool_scale[l])

        q = partial_rotary(u_q.reshape(B, S, DIFF_HEADS, 2, DIFF_HEAD_DIM), cos, sin)
        k = partial_rotary(u_k.reshape(B, S, DIFF_HEADS, 2, DIFF_HEAD_DIM), cos, sin)
        q = q.transpose(0, 2, 3, 1, 4)
        k = k.transpose(0, 2, 3, 1, 4)
        v = u_v.reshape(B, S, DIFF_HEADS, DIFF_V_DIM).transpose(0, 2, 1, 3)
        lam_init = 0.8 - 0.6 * math.exp(-0.3 * l)
        lam = (jnp.exp(jnp.sum(lambda_q1[l].astype(jnp.float32) * lambda_k1[l].astype(jnp.float32)))
               - jnp.exp(jnp.sum(lambda_q2[l].astype(jnp.float32) * lambda_k2[l].astype(jnp.float32)))
               + lam_init)
        o = diff_attention(q, k, v, lam)
        o = rms_norm(o, subln_g[l]) * (1.0 - lam_init)
        y_diff = o.transpose(0, 2, 1, 3).reshape(B, S, DIFF_WIDTH)
        x = x + jnp.concatenate([y_pool, y_diff], axis=-1) @ w_out[l]

        x = x + cross_attention(rms_norm(x, xattn_norm_g[l]), rms_norm(mem, mem_norm_g[l]),
                                w_cq[l], w_ckv[l], w_co[l])

        x = x + moe_ffn(rms_norm(x, ffn_norm_g[l]), w_router[l], b_router[l], w_gu[l], b_gu[l],
                        w_down[l], b_down[l])
    return rms_norm(x, final_norm_g)
```

```python
import functools
import math

import jax
import jax.numpy as jnp
from jax import lax
from jax.experimental import pallas as pl
from jax.experimental.pallas import tpu as pltpu

F32 = jnp.float32
BF16 = jnp.bfloat16
I32 = jnp.int32

D_MODEL = 1024
BATCH = 8
SEQ = 2048
N_TOK = BATCH * SEQ
CHUNK = 64
NORM_EPS = 1e-5
POOL_WIDTH = 512
POOL_WINDOWS = (2, 4, 8, 16)
POOL_GROUP_DIM = 128
MAX_WINDOW = max(POOL_WINDOWS)
DIFF_HEADS = 4
DIFF_HEAD_DIM = 64
DIFF_V_DIM = 128
DIFF_WIDTH = 512
QK_WIDTH = 512
IN_WIDTH = 2048
ROT_DIM = 16
ROPE_THETA = 500000.0
MEM_LEN = 256
X_HEADS = 4
X_HEAD_DIM = 256
N_EXPERTS = 32
TOP_K = 4
D_EXPERT = 1024
SWIGLU_ALPHA = 1.702
SWIGLU_LIMIT = 7.0
LAM_INIT = 0.8 - 0.6 * math.exp(-0.3 * 0)
N_ROWS = N_TOK * TOP_K

LANES = 128

TM_IN = 512
TQ = 256
POOL_ROWS = 512
TM_MID = 512
TM_DISP = 256
TM_GMM = 256
TM_COMB = 256
N_BLOCKS = N_ROWS // TM_GMM
N_ITEMS = N_BLOCKS + N_EXPERTS - 1


def _rms(xf, g):
    ms = jnp.mean(xf * xf, axis=-1, keepdims=True)
    return xf * lax.rsqrt(ms + NORM_EPS) * g


def _dot(a, b):
    return jnp.dot(a, b, preferred_element_type=F32)


def _dot_nt(a, b):
    return lax.dot_general(a, b, (((1,), (1,)), ((), ())), preferred_element_type=F32)


def _in_proj_kernel(x_ref, g_ref, w_ref, cos_ref, sa_ref, sb_ref, up_ref, q_ref, k_ref, v_ref):
    h = _rms(x_ref[...], g_ref[...]).astype(BF16)
    cosf = cos_ref[...]
    sa = sa_ref[...]
    sb = sb_ref[...]
    up_ref[...] = _dot(h, w_ref[:, 0:POOL_WIDTH])
    for off, o_ref, scale in ((POOL_WIDTH, q_ref, DIFF_HEAD_DIM ** -0.5),
                              (POOL_WIDTH + QK_WIDTH, k_ref, 1.0)):
        u = _dot(h, w_ref[:, off:off + QK_WIDTH])
        for hd in range(DIFF_HEADS):
            uh = u[:, hd * LANES:(hd + 1) * LANES]
            r = uh * cosf + pltpu.roll(uh, LANES - 8, 1) * sa + pltpu.roll(uh, 8, 1) * sb
            o_ref[:, hd * LANES:(hd + 1) * LANES] = (r * scale).astype(BF16)
    v_ref[...] = _dot(h, w_ref[:, POOL_WIDTH + 2 * QK_WIDTH:IN_WIDTH]).astype(BF16)


def _in_proj(x2d, g, w_in, cosf, sa, sb):
    n = N_TOK // TM_IN
    tok = lambda i: (i, 0)
    fixed = lambda i: (0, 0)
    return pl.pallas_call(
        _in_proj_kernel,
        grid=(n,),
        in_specs=[
            pl.BlockSpec((TM_IN, D_MODEL), tok),
            pl.BlockSpec((1, D_MODEL), fixed),
            pl.BlockSpec((D_MODEL, IN_WIDTH), fixed),
            pl.BlockSpec((TM_IN, LANES), tok),
            pl.BlockSpec((TM_IN, LANES), tok),
            pl.BlockSpec((TM_IN, LANES), tok),
        ],
        out_specs=[
            pl.BlockSpec((TM_IN, POOL_WIDTH), tok),
            pl.BlockSpec((TM_IN, QK_WIDTH), tok),
            pl.BlockSpec((TM_IN, QK_WIDTH), tok),
            pl.BlockSpec((TM_IN, DIFF_WIDTH), tok),
        ],
        out_shape=[
            jax.ShapeDtypeStruct((N_TOK, POOL_WIDTH), F32),
            jax.ShapeDtypeStruct((N_TOK, QK_WIDTH), BF16),
            jax.ShapeDtypeStruct((N_TOK, QK_WIDTH), BF16),
            jax.ShapeDtypeStruct((N_TOK, DIFF_WIDTH), BF16),
        ],
        compiler_params=pltpu.CompilerParams(
            dimension_semantics=("arbitrary",), vmem_limit_bytes=48 * 1024 * 1024),
        name="in_proj",
    )(x2d, g, w_in, cosf, sa, sb)


def _pool_kernel(u_ref, w_ref, sc_ref, o_ref, pad_ref):
    pad_ref[0:MAX_WINDOW, :] = jnp.zeros((MAX_WINDOW, POOL_WIDTH), F32)
    pad_ref[MAX_WINDOW:, :] = u_ref[0]
    for g, win in enumerate(POOL_WINDOWS):
        lanes = slice(g * POOL_GROUP_DIM, (g + 1) * POOL_GROUP_DIM)
        for c in range(SEQ // POOL_ROWS):
            r0 = c * POOL_ROWS
            u = pad_ref[MAX_WINDOW + r0:MAX_WINDOW + r0 + POOL_ROWS, lanes]
            acc = u
            for j in range(1, win):
                acc = acc + pad_ref[MAX_WINDOW + r0 - j:MAX_WINDOW + r0 - j + POOL_ROWS, lanes]
            t = r0 + lax.broadcasted_iota(I32, (POOL_ROWS, 1), 0)
            cnt = jnp.minimum(t + 1, win).astype(F32)
            mixed = (acc / cnt - u).astype(BF16)
            y = _dot(mixed, w_ref[g]) * sc_ref[:, lanes]
            o_ref[0, r0:r0 + POOL_ROWS, lanes] = y.astype(BF16)


def _pool(u_pool, w_pool, pool_scale):
    return pl.pallas_call(
        _pool_kernel,
        grid=(BATCH,),
        in_specs=[
            pl.BlockSpec((1, SEQ, POOL_WIDTH), lambda b: (b, 0, 0)),
            pl.BlockSpec((len(POOL_WINDOWS), POOL_GROUP_DIM, POOL_GROUP_DIM), lambda b: (0, 0, 0)),
            pl.BlockSpec((1, POOL_WIDTH), lambda b: (0, 0)),
        ],
        out_specs=pl.BlockSpec((1, SEQ, POOL_WIDTH), lambda b: (b, 0, 0)),
        out_shape=jax.ShapeDtypeStruct((BATCH, SEQ, POOL_WIDTH), BF16),
        scratch_shapes=[pltpu.VMEM((SEQ + MAX_WINDOW, POOL_WIDTH), F32)],
        compiler_params=pltpu.CompilerParams(
            dimension_semantics=("arbitrary",), vmem_limit_bytes=48 * 1024 * 1024),
        name="pool_mixer",
    )(u_pool, w_pool, pool_scale)


def _diff_attn_kernel(lam_ref, q_ref, k_ref, v_ref, g_ref, o_ref):
    lv = lam_ref[...]
    e1 = jnp.exp(jnp.sum(lv[0:1] * lv[1:2], axis=-1, keepdims=True))
    e2 = jnp.exp(jnp.sum(lv[2:3] * lv[3:4], axis=-1, keepdims=True))
    lam = e1 - e2 + LAM_INIT
    lane = lax.broadcasted_iota(I32, (TQ, LANES), 1)
    qc = lax.broadcasted_iota(I32, (TQ, TQ), 0) // CHUNK
    kc = lax.broadcasted_iota(I32, (TQ, TQ), 1) // CHUNK
    diag_mask = kc <= qc
    gain = g_ref[...] * (1.0 - LAM_INIT)
    zero = jnp.zeros((), BF16)
    for qi in range(SEQ // TQ):
        q0 = qi * TQ
        qt = q_ref[0, q0:q0 + TQ, :]
        qmaps = (jnp.where(lane < DIFF_HEAD_DIM, qt, zero), jnp.where(lane >= DIFF_HEAD_DIM, qt, zero))
        kd = k_ref[0, q0:q0 + TQ, :]
        s_diag = [jnp.where(diag_mask, _dot_nt(qm, kd), -jnp.inf) for qm in qmaps]
        if qi > 0:
            ka = k_ref[0, 0:q0, :]
            s_past = [_dot_nt(qm, ka) for qm in qmaps]
        p_diag, p_past, inv = [], [], []
        for m in range(2):
            mx = jnp.max(s_diag[m], axis=-1, keepdims=True)
            if qi > 0:
                mx = jnp.maximum(mx, jnp.max(s_past[m], axis=-1, keepdims=True))
            pd = jnp.exp(s_diag[m] - mx)
            den = jnp.sum(pd, axis=-1, keepdims=True)
            p_diag.append(pd)
            if qi > 0:
                pp = jnp.exp(s_past[m] - mx)
                den = den + jnp.sum(pp, axis=-1, keepdims=True)
                p_past.append(pp)
            inv.append(1.0 / den)
        c1 = inv[0]
        c2 = lam * inv[1]
        a_diag = (p_diag[0] * c1 - p_diag[1] * c2).astype(BF16)
        o = _dot(a_diag, v_ref[0, q0:q0 + TQ, :])
        if qi > 0:
            a_past = (p_past[0] * c1 - p_past[1] * c2).astype(BF16)
            o = o + _dot(a_past, v_ref[0, 0:q0, :])
        o_ref[0, q0:q0 + TQ, :] = _rms(o, gain).astype(BF16)


def _diff_attn(lam_vecs, q, k, v, subln_g):
    blk = pl.BlockSpec((1, SEQ, LANES), lambda b, h: (b, 0, h))
    return pl.pallas_call(
        _diff_attn_kernel,
        grid=(BATCH, DIFF_HEADS),
        in_specs=[
            pl.BlockSpec((4, DIFF_HEAD_DIM), lambda b, h: (0, 0)),
            blk, blk, blk,
            pl.BlockSpec((1, DIFF_V_DIM), lambda b, h: (0, 0)),
        ],
        out_specs=blk,
        out_shape=jax.ShapeDtypeStruct((BATCH, SEQ, DIFF_WIDTH), BF16),
        compiler_params=pltpu.CompilerParams(
            dimension_semantics=("arbitrary", "arbitrary"), vmem_limit_bytes=48 * 1024 * 1024),
        name="diff_attn",
    )(lam_vecs, q, k, v, subln_g)


def _mem_kv_kernel(m_ref, g_ref, w_ref, k_ref, v_ref):
    h = _rms(m_ref[0], g_ref[...]).astype(BF16)
    k_ref[0] = _dot(h, w_ref[:, 0:D_MODEL]).astype(BF16)
    v_ref[0] = _dot(h, w_ref[:, D_MODEL:2 * D_MODEL]).astype(BF16)


def _mem_kv(mem, g, w_ckv):
    blk = pl.BlockSpec((1, MEM_LEN, D_MODEL), lambda b: (b, 0, 0))
    return pl.pallas_call(
        _mem_kv_kernel,
        grid=(BATCH,),
        in_specs=[blk, pl.BlockSpec((1, D_MODEL), lambda b: (0, 0)),
                  pl.BlockSpec((D_MODEL, 2 * D_MODEL), lambda b: (0, 0))],
        out_specs=[blk, blk],
        out_shape=[jax.ShapeDtypeStruct((BATCH, MEM_LEN, D_MODEL), BF16)] * 2,
        compiler_params=pltpu.CompilerParams(
            dimension_semantics=("arbitrary",), vmem_limit_bytes=48 * 1024 * 1024),
        name="mem_kv",
    )(mem, g, w_ckv)


def _mid_kernel(x_ref, yp_ref, yd_ref, wo_ref, g2_ref, wcq_ref, kx_ref, vx_ref, wco_ref, g3_ref,
                wrh_ref, wrl_ref, br_ref, tri_ref,
                x2_ref, xn_ref, idx_ref, gate_ref, rank_ref, cnt_ref, attn_ref, base_ref):
    @pl.when(pl.program_id(0) == 0)
    def _():
        base_ref[...] = jnp.zeros_like(base_ref)

    x1 = x_ref[...] + _dot(yp_ref[...], wo_ref[0:POOL_WIDTH, :]) + _dot(yd_ref[...], wo_ref[POOL_WIDTH:, :])
    h2 = _rms(x1, g2_ref[...]).astype(BF16)
    qx = (_dot(h2, wcq_ref[...]) * (X_HEAD_DIM ** -0.5)).astype(BF16)
    for hd in range(X_HEADS):
        cols = slice(hd * X_HEAD_DIM, (hd + 1) * X_HEAD_DIM)
        s = _dot_nt(qx[:, cols], kx_ref[0, :, cols])
        p = jnp.exp(s - jnp.max(s, axis=-1, keepdims=True))
        p = p / jnp.sum(p, axis=-1, keepdims=True)
        attn_ref[:, cols] = _dot(p.astype(BF16), vx_ref[0, :, cols]).astype(BF16)
    x2 = x1 + _dot(attn_ref[...], wco_ref[...])
    x2_ref[...] = x2
    xn = _rms(x2, g3_ref[...])
    xn_ref[...] = xn

    xh = xn.astype(BF16)
    xl = (xn - xh.astype(F32)).astype(BF16)
    logits = (_dot_nt(wrh_ref[...], xh) + _dot_nt(wrh_ref[...], xl) + _dot_nt(wrl_ref[...], xh)
              + br_ref[...])
    eidx = lax.broadcasted_iota(I32, (N_EXPERTS, TM_MID), 0).astype(F32)
    vals, hots = [], []
    for k in range(TOP_K):
        mx = jnp.max(logits, axis=0, keepdims=True)
        sel = jnp.min(jnp.where(logits == mx, eidx, float(N_EXPERTS)), axis=0, keepdims=True)
        hot = eidx == sel
        idx_ref[k:k + 1, :] = sel.astype(I32)
        vals.append(mx)
        hots.append(hot)
        logits = jnp.where(hot, -jnp.inf, logits)
    ex = [jnp.exp(v - vals[0]) for v in vals]
    den = ex[0] + ex[1] + ex[2] + ex[3]
    for k in range(TOP_K):
        gate_ref[k:k + 1, :] = ex[k] / den

    chosen = jnp.zeros((N_EXPERTS, TM_MID), F32)
    for k in range(TOP_K):
        chosen = chosen + jnp.where(hots[k], 1.0, 0.0)
    before = _dot(chosen.astype(BF16), tri_ref[...]) + base_ref[:, 0:1]
    for k in range(TOP_K):
        rank_ref[k:k + 1, :] = jnp.sum(jnp.where(hots[k], before, 0.0), axis=0, keepdims=True).astype(I32)
    base_ref[...] = base_ref[...] + jnp.sum(chosen, axis=1, keepdims=True)
    cnt_ref[...] = base_ref[...]


def _mid(x2d, y_pool, y_diff, w_out, g2, w_cq, kx, vx, w_co, g3, wr_hi, wr_lo, b_r, tri):
    n = N_TOK // TM_MID
    per_batch = SEQ // TM_MID
    tok = lambda i: (i, 0)
    fixed = lambda i: (0, 0)
    kv = pl.BlockSpec((1, MEM_LEN, D_MODEL), lambda i: (i // per_batch, 0, 0))
    row4 = pl.BlockSpec((TOP_K, TM_MID), lambda i: (0, i))
    return pl.pallas_call(
        _mid_kernel,
        grid=(n,),
        in_specs=[
            pl.BlockSpec((TM_MID, D_MODEL), tok),
            pl.BlockSpec((TM_MID, POOL_WIDTH), tok),
            pl.BlockSpec((TM_MID, DIFF_WIDTH), tok),
            pl.BlockSpec((D_MODEL, D_MODEL), fixed),
            pl.BlockSpec((1, D_MODEL), fixed),
            pl.BlockSpec((D_MODEL, D_MODEL), fixed),
            kv, kv,
            pl.BlockSpec((D_MODEL, D_MODEL), fixed),
            pl.BlockSpec((1, D_MODEL), fixed),
            pl.BlockSpec((N_EXPERTS, D_MODEL), fixed),
            pl.BlockSpec((N_EXPERTS, D_MODEL), fixed),
            pl.BlockSpec((N_EXPERTS, 1), fixed),
            pl.BlockSpec((TM_MID, TM_MID), fixed),
        ],
        out_specs=[
            pl.BlockSpec((TM_MID, D_MODEL), tok),
            pl.BlockSpec((TM_MID, D_MODEL), tok),
            row4, row4, row4,
            pl.BlockSpec((N_EXPERTS, LANES), fixed),
        ],
        out_shape=[
            jax.ShapeDtypeStruct((N_TOK, D_MODEL), F32),
            jax.ShapeDtypeStruct((N_TOK, D_MODEL), F32),
            jax.ShapeDtypeStruct((TOP_K, N_TOK), I32),
            jax.ShapeDtypeStruct((TOP_K, N_TOK), F32),
            jax.ShapeDtypeStruct((TOP_K, N_TOK), I32),
            jax.ShapeDtypeStruct((N_EXPERTS, LANES), F32),
        ],
        scratch_shapes=[pltpu.VMEM((TM_MID, D_MODEL), BF16), pltpu.VMEM((N_EXPERTS, LANES), F32)],
        compiler_params=pltpu.CompilerParams(
            dimension_semantics=("arbitrary",), vmem_limit_bytes=56 * 1024 * 1024),
        name="mid_block",
    )(x2d, y_pool, y_diff, w_out, g2, w_cq, kx, vx, w_co, g3, wr_hi, wr_lo, b_r, tri)


def _row_copy(src_ref, src_row, dst_ref, dst_row, sem):
    return pltpu.make_async_copy(src_ref.at[pl.ds(src_row, 1)], dst_ref.at[pl.ds(dst_row, 1)], sem)


def _dispatch_kernel(dest_ref, xn_ref, xs_ref, sem):
    t0 = pl.program_id(0) * TM_DISP

    def copies(r):
        return [_row_copy(xn_ref, r, xs_ref, dest_ref[(t0 + r) * TOP_K + k], sem) for k in range(TOP_K)]

    def issue(r, c):
        for cp in copies(r):
            cp.start()
        return c

    def drain(r, c):
        for cp in copies(r):
            cp.wait()
        return c

    lax.fori_loop(0, TM_DISP, issue, 0)
    lax.fori_loop(0, TM_DISP, drain, 0)


def _dispatch(dest_flat, xn):
    return pl.pallas_call(
        _dispatch_kernel,
        grid_spec=pltpu.PrefetchScalarGridSpec(
            num_scalar_prefetch=1,
            grid=(N_TOK // TM_DISP,),
            in_specs=[pl.BlockSpec((TM_DISP, D_MODEL), lambda i, d: (i, 0))],
            out_specs=pl.BlockSpec(memory_space=pl.ANY),
            scratch_shapes=[pltpu.SemaphoreType.DMA(())],
        ),
        out_shape=jax.ShapeDtypeStruct((N_ROWS, D_MODEL), F32),
        compiler_params=pltpu.CompilerParams(dimension_semantics=("arbitrary",)),
        name="moe_dispatch",
    )(dest_flat, xn)


def _gmm_kernel(blk_ref, exp_ref, lo_ref, hi_ref, first_ref, xs_ref, wgu_ref, bgu_ref, wd_ref, bd_ref, o_ref):
    w = pl.program_id(0)
    lo = lo_ref[w]
    hi = hi_ref[w]

    @pl.when(hi > lo)
    def _():
        xb = xs_ref[...].astype(BF16)
        gate = _dot(xb, wgu_ref[0, :, 0:D_EXPERT]) + bgu_ref[0, :, 0:D_EXPERT]
        up = _dot(xb, wgu_ref[0, :, D_EXPERT:]) + bgu_ref[0, :, D_EXPERT:]
        gate = jnp.minimum(gate, SWIGLU_LIMIT)
        up = jnp.clip(up, -SWIGLU_LIMIT, SWIGLU_LIMIT)
        hid = (up + 1.0) * gate * jax.nn.sigmoid(SWIGLU_ALPHA * gate)
        y = _dot(hid.astype(BF16), wd_ref[0]) + bd_ref[0]
        row = lax.broadcasted_iota(I32, (TM_GMM, 1), 0)
        mine = (row >= lo) & (row < hi)

        @pl.when(first_ref[w] == 1)
        def _():
            o_ref[...] = jnp.where(mine, y, 0.0)

        @pl.when(first_ref[w] == 0)
        def _():
            o_ref[...] = jnp.where(mine, y, o_ref[...])


def _gmm(item_blk, item_exp, item_lo, item_hi, item_first, xs, w_gu, b_gu, w_down, b_down):
    rows = lambda w, blk, e, lo, hi, f: (blk[w], 0)
    per_e = lambda w, blk, e, lo, hi, f: (e[w], 0, 0)
    return pl.pallas_call(
        _gmm_kernel,
        grid_spec=pltpu.PrefetchScalarGridSpec(
            num_scalar_prefetch=5,
            grid=(N_ITEMS,),
            in_specs=[
                pl.BlockSpec((TM_GMM, D_MODEL), rows),
                pl.BlockSpec((1, D_MODEL, 2 * D_EXPERT), per_e),
                pl.BlockSpec((1, 1, 2 * D_EXPERT), per_e),
                pl.BlockSpec((1, D_EXPERT, D_MODEL), per_e),
                pl.BlockSpec((1, 1, D_MODEL), per_e),
            ],
            out_specs=pl.BlockSpec((TM_GMM, D_MODEL), rows),
        ),
        out_shape=jax.ShapeDtypeStruct((N_ROWS, D_MODEL), F32),
        compiler_params=pltpu.CompilerParams(
            dimension_semantics=("arbitrary",), vmem_limit_bytes=56 * 1024 * 1024),
        name="moe_experts",
    )(item_blk, item_exp, item_lo, item_hi, item_first, xs, w_gu, b_gu, w_down, b_down)


def _combine_kernel(dest_ref, ys_ref, x2_ref, gate_ref, g_ref, o_ref, buf_ref, sem):
    t0 = pl.program_id(0) * TM_COMB

    def copies(r):
        return [_row_copy(ys_ref, dest_ref[(t0 + r) * TOP_K + k], buf_ref.at[k], r, sem) for k in range(TOP_K)]

    def issue(r, c):
        for cp in copies(r):
            cp.start()
        return c

    def drain(r, c):
        for cp in copies(r):
            cp.wait()
        return c

    lax.fori_loop(0, TM_COMB, issue, 0)
    lax.fori_loop(0, TM_COMB, drain, 0)
    gates = gate_ref[...]
    y = x2_ref[...]
    for k in range(TOP_K):
        y = y + buf_ref[k] * gates[:, k:k + 1]
    o_ref[...] = _rms(y, g_ref[...])


def _combine(dest_flat, ys, x2, gates_t, g):
    tok = lambda i, d: (i, 0)
    return pl.pallas_call(
        _combine_kernel,
        grid_spec=pltpu.PrefetchScalarGridSpec(
            num_scalar_prefetch=1,
            grid=(N_TOK // TM_COMB,),
            in_specs=[
                pl.BlockSpec(memory_space=pl.ANY),
                pl.BlockSpec((TM_COMB, D_MODEL), tok),
                pl.BlockSpec((TM_COMB, TOP_K), tok),
                pl.BlockSpec((1, D_MODEL), lambda i, d: (0, 0)),
            ],
            out_specs=pl.BlockSpec((TM_COMB, D_MODEL), tok),
            scratch_shapes=[pltpu.VMEM((TOP_K, TM_COMB, D_MODEL), F32), pltpu.SemaphoreType.DMA(())],
        ),
        out_shape=jax.ShapeDtypeStruct((N_TOK, D_MODEL), F32),
        compiler_params=pltpu.CompilerParams(
            dimension_semantics=("arbitrary",), vmem_limit_bytes=48 * 1024 * 1024),
        name="moe_combine",
    )(dest_flat, ys, x2, gates_t, g)


def _rotary_lane_tables(positions):
    half = ROT_DIM // 2
    inv_freq = ROPE_THETA ** (-jnp.arange(0, ROT_DIM, 2, dtype=F32) / ROT_DIM)
    ang = positions.reshape(N_TOK, 1).astype(F32) * inv_freq
    cos, sin = jnp.cos(ang), jnp.sin(ang)
    rest = DIFF_HEAD_DIM - ROT_DIM
    ones = jnp.ones((N_TOK, rest), F32)
    zeros = jnp.zeros((N_TOK, rest), F32)
    zh = jnp.zeros((N_TOK, half), F32)
    reps = LANES // DIFF_HEAD_DIM
    cosf = jnp.tile(jnp.concatenate([cos, cos, ones], axis=1), (1, reps))
    sa = jnp.tile(jnp.concatenate([-sin, zh, zeros], axis=1), (1, reps))
    sb = jnp.tile(jnp.concatenate([zh, sin, zeros], axis=1), (1, reps))
    return cosf, sa, sb


def _work_items(counts):
    ends = jnp.cumsum(counts)
    starts = ends - counts
    first_blk = starts // TM_GMM
    last_blk = jnp.maximum(ends - 1, 0) // TM_GMM
    n_items = jnp.where(counts > 0, last_blk - first_blk + 1, 0)
    item_end = jnp.cumsum(n_items)
    item_start = item_end - n_items
    total = item_end[-1]
    w = jnp.arange(N_ITEMS, dtype=I32)
    valid = w < total
    e = jnp.minimum(jnp.searchsorted(item_end, jnp.minimum(w, total - 1), side='right'), N_EXPERTS - 1).astype(I32)
    blk = jnp.where(valid, first_blk[e] + (w - item_start[e]), N_BLOCKS - 1).astype(I32)
    lo = jnp.where(valid, jnp.maximum(starts[e] - blk * TM_GMM, 0), 0).astype(I32)
    hi = jnp.where(valid, jnp.minimum(ends[e] - blk * TM_GMM, TM_GMM), 0).astype(I32)
    prev = jnp.concatenate([jnp.full((1,), -1, I32), blk[:-1]])
    first = (blk != prev).astype(I32)
    return blk, e, lo, hi, first, starts


def kernel(x, positions, mem, attn_norm_g, w_in, w_pool, pool_scale, lambda_q1, lambda_k1, lambda_q2, lambda_k2, subln_g, w_out, xattn_norm_g, mem_norm_g, w_cq, w_ckv, w_co, ffn_norm_g, w_router, b_router, w_gu, b_gu, w_down, b_down, final_norm_g):
    l = 0
    x2d = x.reshape(N_TOK, D_MODEL)
    cosf, sa, sb = _rotary_lane_tables(positions)

    u_pool, q, k, v = _in_proj(x2d, attn_norm_g[l].reshape(1, D_MODEL), w_in[l].astype(BF16), cosf, sa, sb)
    y_pool = _pool(u_pool.reshape(BATCH, SEQ, POOL_WIDTH), w_pool[l].astype(BF16),
                   pool_scale[l].reshape(1, POOL_WIDTH))
    lam_vecs = jnp.stack([lambda_q1[l], lambda_k1[l], lambda_q2[l], lambda_k2[l]]).astype(F32)
    y_diff = _diff_attn(lam_vecs, q.reshape(BATCH, SEQ, QK_WIDTH), k.reshape(BATCH, SEQ, QK_WIDTH),
                        v.reshape(BATCH, SEQ, DIFF_WIDTH), subln_g[l].reshape(1, DIFF_V_DIM))
    kx, vx = _mem_kv(mem, mem_norm_g[l].reshape(1, D_MODEL), w_ckv[l].astype(BF16))

    wr_t = w_router[l].T
    wr_hi = wr_t.astype(BF16)
    wr_lo = (wr_t - wr_hi.astype(F32)).astype(BF16)
    tri = (jnp.arange(TM_MID)[:, None] < jnp.arange(TM_MID)[None, :]).astype(BF16)
    x2, xn, top_idx, gates, rank, cnt = _mid(
        x2d, y_pool.reshape(N_TOK, POOL_WIDTH), y_diff.reshape(N_TOK, DIFF_WIDTH), w_out[l].astype(BF16),
        xattn_norm_g[l].reshape(1, D_MODEL), w_cq[l].astype(BF16), kx, vx, w_co[l].astype(BF16),
        ffn_norm_g[l].reshape(1, D_MODEL), wr_hi, wr_lo, b_router[l].reshape(N_EXPERTS, 1), tri)

    counts = cnt[:, 0].astype(I32)
    item_blk, item_exp, item_lo, item_hi, item_first, starts = _work_items(counts)
    dest_flat = (starts[top_idx] + rank).T.reshape(N_ROWS)

    xs = _dispatch(dest_flat, xn)
    ys = _gmm(item_blk, item_exp, item_lo, item_hi, item_first, xs,
              w_gu[l].astype(BF16), b_gu[l].reshape(N_EXPERTS, 1, 2 * D_EXPERT),
              w_down[l].astype(BF16), b_down[l].reshape(N_EXPERTS, 1, D_MODEL))
    out = _combine(dest_flat, ys, x2, gates.T, final_norm_g.reshape(1, D_MODEL))
    return out.reshape(BATCH, SEQ, D_MODEL)
```

```python
import functools
import math

import jax
import jax.numpy as jnp
from jax import lax
from jax.experimental import pallas as pl
from jax.experimental.pallas import tpu as pltpu

F32 = jnp.float32
BF16 = jnp.bfloat16
I32 = jnp.int32

D_MODEL = 1024
BATCH = 8
SEQ = 2048
N_TOK = BATCH * SEQ
CHUNK = 64
NORM_EPS = 1e-5
POOL_WIDTH = 512
POOL_WINDOWS = (2, 4, 8, 16)
POOL_GROUP_DIM = 128
MAX_WINDOW = max(POOL_WINDOWS)
DIFF_HEADS = 4
DIFF_HEAD_DIM = 64
DIFF_V_DIM = 128
DIFF_WIDTH = 512
QK_WIDTH = 512
IN_WIDTH = 2048
ROT_DIM = 16
ROPE_THETA = 500000.0
MEM_LEN = 256
X_HEADS = 4
X_HEAD_DIM = 256
N_EXPERTS = 32
TOP_K = 4
D_EXPERT = 1024
SWIGLU_ALPHA = 1.702
SWIGLU_LIMIT = 7.0
LAM_INIT = 0.8 - 0.6 * math.exp(-0.3 * 0)
N_ROWS = N_TOK * TOP_K

LANES = 128

TM_IN = 512
TQ = 256
POOL_ROWS = 512
TM_MID = 512
TM_DISP = 256
TM_GMM = 256
TM_COMB = 256
ROW_UNROLL = 8
N_BLOCKS = N_ROWS // TM_GMM
N_ITEMS = N_BLOCKS + N_EXPERTS - 1


def _rms(xf, g):
    ms = jnp.mean(xf * xf, axis=-1, keepdims=True)
    return xf * lax.rsqrt(ms + NORM_EPS) * g


def _dot(a, b):
    return jnp.dot(a, b, preferred_element_type=F32)


def _dot_nt(a, b):
    return lax.dot_general(a, b, (((1,), (1,)), ((), ())), preferred_element_type=F32)


def _in_proj_kernel(x_ref, g_ref, w_ref, pos_ref, rot_ref, up_ref, q_ref, k_ref, v_ref):
    h = _rms(x_ref[...], g_ref[...]).astype(BF16)
    ang = pos_ref[...].astype(F32) * rot_ref[0:1, :]
    cosf = jnp.cos(ang)
    sinf = jnp.sin(ang)
    sa = sinf * rot_ref[1:2, :]
    sb = sinf * rot_ref[2:3, :]
    up_ref[...] = _dot(h, w_ref[:, 0:POOL_WIDTH])
    for off, o_ref, scale in ((POOL_WIDTH, q_ref, DIFF_HEAD_DIM ** -0.5),
                              (POOL_WIDTH + QK_WIDTH, k_ref, 1.0)):
        u = _dot(h, w_ref[:, off:off + QK_WIDTH])
        for hd in range(DIFF_HEADS):
            uh = u[:, hd * LANES:(hd + 1) * LANES]
            r = uh * cosf + pltpu.roll(uh, LANES - 8, 1) * sa + pltpu.roll(uh, 8, 1) * sb
            o_ref[:, hd * LANES:(hd + 1) * LANES] = (r * scale).astype(BF16)
    v_ref[...] = _dot(h, w_ref[:, POOL_WIDTH + 2 * QK_WIDTH:IN_WIDTH]).astype(BF16)


def _in_proj(x2d, g, w_in, pos, rot_tab):
    n = N_TOK // TM_IN
    tok = lambda i: (i, 0)
    fixed = lambda i: (0, 0)
    return pl.pallas_call(
        _in_proj_kernel,
        grid=(n,),
        in_specs=[
            pl.BlockSpec((TM_IN, D_MODEL), tok),
            pl.BlockSpec((1, D_MODEL), fixed),
            pl.BlockSpec((D_MODEL, IN_WIDTH), fixed),
            pl.BlockSpec((TM_IN, 1), tok),
            pl.BlockSpec((8, LANES), fixed),
        ],
        out_specs=[
            pl.BlockSpec((TM_IN, POOL_WIDTH), tok),
            pl.BlockSpec((TM_IN, QK_WIDTH), tok),
            pl.BlockSpec((TM_IN, QK_WIDTH), tok),
            pl.BlockSpec((TM_IN, DIFF_WIDTH), tok),
        ],
        out_shape=[
            jax.ShapeDtypeStruct((N_TOK, POOL_WIDTH), F32),
            jax.ShapeDtypeStruct((N_TOK, QK_WIDTH), BF16),
            jax.ShapeDtypeStruct((N_TOK, QK_WIDTH), BF16),
            jax.ShapeDtypeStruct((N_TOK, DIFF_WIDTH), BF16),
        ],
        compiler_params=pltpu.CompilerParams(
            dimension_semantics=("arbitrary",), vmem_limit_bytes=48 * 1024 * 1024),
        name="in_proj",
    )(x2d, g, w_in, pos, rot_tab)


def _pool_kernel(u_ref, w_ref, sc_ref, o_ref, pad_ref):
    pad_ref[0:MAX_WINDOW, :] = jnp.zeros((MAX_WINDOW, POOL_WIDTH), F32)
    pad_ref[MAX_WINDOW:, :] = u_ref[0]
    for g, win in enumerate(POOL_WINDOWS):
        lanes = slice(g * POOL_GROUP_DIM, (g + 1) * POOL_GROUP_DIM)
        for c in range(SEQ // POOL_ROWS):
            r0 = c * POOL_ROWS
            u = pad_ref[MAX_WINDOW + r0:MAX_WINDOW + r0 + POOL_ROWS, lanes]
            acc = u
            for j in range(1, win):
                acc = acc + pad_ref[MAX_WINDOW + r0 - j:MAX_WINDOW + r0 - j + POOL_ROWS, lanes]
            t = r0 + lax.broadcasted_iota(I32, (POOL_ROWS, 1), 0)
            cnt = jnp.minimum(t + 1, win).astype(F32)
            mixed = (acc / cnt - u).astype(BF16)
            y = _dot(mixed, w_ref[g]) * sc_ref[:, lanes]
            o_ref[0, r0:r0 + POOL_ROWS, lanes] = y.astype(BF16)


def _pool(u_pool, w_pool, pool_scale):
    return pl.pallas_call(
        _pool_kernel,
        grid=(BATCH,),
        in_specs=[
            pl.BlockSpec((1, SEQ, POOL_WIDTH), lambda b: (b, 0, 0)),
            pl.BlockSpec((len(POOL_WINDOWS), POOL_GROUP_DIM, POOL_GROUP_DIM), lambda b: (0, 0, 0)),
            pl.BlockSpec((1, POOL_WIDTH), lambda b: (0, 0)),
        ],
        out_specs=pl.BlockSpec((1, SEQ, POOL_WIDTH), lambda b: (b, 0, 0)),
        out_shape=jax.ShapeDtypeStruct((BATCH, SEQ, POOL_WIDTH), BF16),
        scratch_shapes=[pltpu.VMEM((SEQ + MAX_WINDOW, POOL_WIDTH), F32)],
        compiler_params=pltpu.CompilerParams(
            dimension_semantics=("arbitrary",), vmem_limit_bytes=48 * 1024 * 1024),
        name="pool_mixer",
    )(u_pool, w_pool, pool_scale)


def _diff_attn_kernel(lam_ref, q_ref, k_ref, v_ref, g_ref, o_ref):
    lv = lam_ref[...]
    e1 = jnp.exp(jnp.sum(lv[0:1] * lv[1:2], axis=-1, keepdims=True))
    e2 = jnp.exp(jnp.sum(lv[2:3] * lv[3:4], axis=-1, keepdims=True))
    lam = e1 - e2 + LAM_INIT
    lane = lax.broadcasted_iota(I32, (TQ, LANES), 1)
    qc = lax.broadcasted_iota(I32, (TQ, TQ), 0) // CHUNK
    kc = lax.broadcasted_iota(I32, (TQ, TQ), 1) // CHUNK
    diag_mask = kc <= qc
    gain = g_ref[...] * (1.0 - LAM_INIT)
    zero = jnp.zeros((), BF16)
    for qi in range(SEQ // TQ):
        q0 = qi * TQ
        qt = q_ref[0, q0:q0 + TQ, :]
        qmaps = (jnp.where(lane < DIFF_HEAD_DIM, qt, zero), jnp.where(lane >= DIFF_HEAD_DIM, qt, zero))
        kd = k_ref[0, q0:q0 + TQ, :]
        s_diag = [jnp.where(diag_mask, _dot_nt(qm, kd), -jnp.inf) for qm in qmaps]
        if qi > 0:
            ka = k_ref[0, 0:q0, :]
            s_past = [_dot_nt(qm, ka) for qm in qmaps]
        p_diag, p_past, inv = [], [], []
        for m in range(2):
            mx = jnp.max(s_diag[m], axis=-1, keepdims=True)
            if qi > 0:
                mx = jnp.maximum(mx, jnp.max(s_past[m], axis=-1, keepdims=True))
            pd = jnp.exp(s_diag[m] - mx)
            den = jnp.sum(pd, axis=-1, keepdims=True)
            p_diag.append(pd)
            if qi > 0:
                pp = jnp.exp(s_past[m] - mx)
                den = den + jnp.sum(pp, axis=-1, keepdims=True)
                p_past.append(pp)
            inv.append(1.0 / den)
        c1 = inv[0]
        c2 = lam * inv[1]
        a_diag = (p_diag[0] * c1 - p_diag[1] * c2).astype(BF16)
        o = _dot(a_diag, v_ref[0, q0:q0 + TQ, :])
        if qi > 0:
            a_past = (p_past[0] * c1 - p_past[1] * c2).astype(BF16)
            o = o + _dot(a_past, v_ref[0, 0:q0, :])
        o_ref[0, q0:q0 + TQ, :] = _rms(o, gain).astype(BF16)


def _diff_attn(lam_vecs, q, k, v, subln_g):
    blk = pl.BlockSpec((1, SEQ, LANES), lambda b, h: (b, 0, h))
    return pl.pallas_call(
        _diff_attn_kernel,
        grid=(BATCH, DIFF_HEADS),
        in_specs=[
            pl.BlockSpec((4, DIFF_HEAD_DIM), lambda b, h: (0, 0)),
            blk, blk, blk,
            pl.BlockSpec((1, DIFF_V_DIM), lambda b, h: (0, 0)),
        ],
        out_specs=blk,
        out_shape=jax.ShapeDtypeStruct((BATCH, SEQ, DIFF_WIDTH), BF16),
        compiler_params=pltpu.CompilerParams(
            dimension_semantics=("arbitrary", "arbitrary"), vmem_limit_bytes=48 * 1024 * 1024),
        name="diff_attn",
    )(lam_vecs, q, k, v, subln_g)


def _mem_kv_kernel(m_ref, g_ref, w_ref, k_ref, v_ref):
    h = _rms(m_ref[0], g_ref[...]).astype(BF16)
    k_ref[0] = _dot(h, w_ref[:, 0:D_MODEL]).astype(BF16)
    v_ref[0] = _dot(h, w_ref[:, D_MODEL:2 * D_MODEL]).astype(BF16)


def _mem_kv(mem, g, w_ckv):
    blk = pl.BlockSpec((1, MEM_LEN, D_MODEL), lambda b: (b, 0, 0))
    return pl.pallas_call(
        _mem_kv_kernel,
        grid=(BATCH,),
        in_specs=[blk, pl.BlockSpec((1, D_MODEL), lambda b: (0, 0)),
                  pl.BlockSpec((D_MODEL, 2 * D_MODEL), lambda b: (0, 0))],
        out_specs=[blk, blk],
        out_shape=[jax.ShapeDtypeStruct((BATCH, MEM_LEN, D_MODEL), BF16)] * 2,
        compiler_params=pltpu.CompilerParams(
            dimension_semantics=("arbitrary",), vmem_limit_bytes=48 * 1024 * 1024),
        name="mem_kv",
    )(mem, g, w_ckv)


def _mid_kernel(x_ref, yp_ref, yd_ref, wo_ref, g2_ref, wcq_ref, kx_ref, vx_ref, wco_ref, g3_ref,
                wrh_ref, wrl_ref, br_ref, tri_ref,
                x2_ref, xn_ref, idx_ref, gate_ref, rank_ref, cnt_ref, attn_ref, base_ref):
    @pl.when(pl.program_id(0) == 0)
    def _():
        base_ref[...] = jnp.zeros_like(base_ref)

    x1 = x_ref[...] + _dot(yp_ref[...], wo_ref[0:POOL_WIDTH, :]) + _dot(yd_ref[...], wo_ref[POOL_WIDTH:, :])
    h2 = _rms(x1, g2_ref[...]).astype(BF16)
    qx = (_dot(h2, wcq_ref[...]) * (X_HEAD_DIM ** -0.5)).astype(BF16)
    for hd in range(X_HEADS):
        cols = slice(hd * X_HEAD_DIM, (hd + 1) * X_HEAD_DIM)
        s = _dot_nt(qx[:, cols], kx_ref[0, :, cols])
        p = jnp.exp(s - jnp.max(s, axis=-1, keepdims=True))
        p = p / jnp.sum(p, axis=-1, keepdims=True)
        attn_ref[:, cols] = _dot(p.astype(BF16), vx_ref[0, :, cols]).astype(BF16)
    x2 = x1 + _dot(attn_ref[...], wco_ref[...])
    x2_ref[...] = x2
    xn = _rms(x2, g3_ref[...])
    xn_ref[...] = xn

    xh = xn.astype(BF16)
    xl = (xn - xh.astype(F32)).astype(BF16)
    logits = (_dot_nt(wrh_ref[...], xh) + _dot_nt(wrh_ref[...], xl) + _dot_nt(wrl_ref[...], xh)
              + br_ref[...])
    eidx = lax.broadcasted_iota(I32, (N_EXPERTS, TM_MID), 0).astype(F32)
    vals, hots = [], []
    for k in range(TOP_K):
        mx = jnp.max(logits, axis=0, keepdims=True)
        sel = jnp.min(jnp.where(logits == mx, eidx, float(N_EXPERTS)), axis=0, keepdims=True)
        hot = eidx == sel
        idx_ref[k:k + 1, :] = sel.astype(I32)
        vals.append(mx)
        hots.append(hot)
        logits = jnp.where(hot, -jnp.inf, logits)
    ex = [jnp.exp(v - vals[0]) for v in vals]
    den = ex[0] + ex[1] + ex[2] + ex[3]
    for k in range(TOP_K):
        gate_ref[k:k + 1, :] = ex[k] / den

    chosen = jnp.zeros((N_EXPERTS, TM_MID), F32)
    for k in range(TOP_K):
        chosen = chosen + jnp.where(hots[k], 1.0, 0.0)
    before = _dot(chosen.astype(BF16), tri_ref[...]) + base_ref[:, 0:1]
    for k in range(TOP_K):
        rank_ref[k:k + 1, :] = jnp.sum(jnp.where(hots[k], before, 0.0), axis=0, keepdims=True).astype(I32)
    base_ref[...] = base_ref[...] + jnp.sum(chosen, axis=1, keepdims=True)
    cnt_ref[...] = base_ref[...]


def _mid(x2d, y_pool, y_diff, w_out, g2, w_cq, kx, vx, w_co, g3, wr_hi, wr_lo, b_r, tri):
    n = N_TOK // TM_MID
    per_batch = SEQ // TM_MID
    tok = lambda i: (i, 0)
    fixed = lambda i: (0, 0)
    kv = pl.BlockSpec((1, MEM_LEN, D_MODEL), lambda i: (i // per_batch, 0, 0))
    row4 = pl.BlockSpec((TOP_K, TM_MID), lambda i: (0, i))
    return pl.pallas_call(
        _mid_kernel,
        grid=(n,),
        in_specs=[
            pl.BlockSpec((TM_MID, D_MODEL), tok),
            pl.BlockSpec((TM_MID, POOL_WIDTH), tok),
            pl.BlockSpec((TM_MID, DIFF_WIDTH), tok),
            pl.BlockSpec((D_MODEL, D_MODEL), fixed),
            pl.BlockSpec((1, D_MODEL), fixed),
            pl.BlockSpec((D_MODEL, D_MODEL), fixed),
            kv, kv,
            pl.BlockSpec((D_MODEL, D_MODEL), fixed),
            pl.BlockSpec((1, D_MODEL), fixed),
            pl.BlockSpec((N_EXPERTS, D_MODEL), fixed),
            pl.BlockSpec((N_EXPERTS, D_MODEL), fixed),
            pl.BlockSpec((N_EXPERTS, 1), fixed),
            pl.BlockSpec((TM_MID, TM_MID), fixed),
        ],
        out_specs=[
            pl.BlockSpec((TM_MID, D_MODEL), tok),
            pl.BlockSpec((TM_MID, D_MODEL), tok),
            row4, row4, row4,
            pl.BlockSpec((N_EXPERTS, LANES), fixed),
        ],
        out_shape=[
            jax.ShapeDtypeStruct((N_TOK, D_MODEL), F32),
            jax.ShapeDtypeStruct((N_TOK, D_MODEL), F32),
            jax.ShapeDtypeStruct((TOP_K, N_TOK), I32),
            jax.ShapeDtypeStruct((TOP_K, N_TOK), F32),
            jax.ShapeDtypeStruct((TOP_K, N_TOK), I32),
            jax.ShapeDtypeStruct((N_EXPERTS, LANES), F32),
        ],
        scratch_shapes=[pltpu.VMEM((TM_MID, D_MODEL), BF16), pltpu.VMEM((N_EXPERTS, LANES), F32)],
        compiler_params=pltpu.CompilerParams(
            dimension_semantics=("arbitrary",), vmem_limit_bytes=56 * 1024 * 1024),
        name="mid_block",
    )(x2d, y_pool, y_diff, w_out, g2, w_cq, kx, vx, w_co, g3, wr_hi, wr_lo, b_r, tri)


def _row_copy(src_ref, src_row, dst_ref, dst_row, sem):
    return pltpu.make_async_copy(src_ref.at[pl.ds(src_row, 1)], dst_ref.at[pl.ds(dst_row, 1)], sem)


def _for_rows(n_rows, fn):
    def body(g, c):
        for j in range(ROW_UNROLL):
            fn(g * ROW_UNROLL + j)
        return c

    lax.fori_loop(0, n_rows // ROW_UNROLL, body, 0)


def _dispatch_kernel(dest_ref, xn_ref, xs_ref, sem):
    t0 = pl.program_id(0) * TM_DISP

    def copies(r):
        return [_row_copy(xn_ref, r, xs_ref, dest_ref[(t0 + r) * TOP_K + k], sem) for k in range(TOP_K)]

    def issue(r):
        for cp in copies(r):
            cp.start()

    def drain(r):
        for cp in copies(r):
            cp.wait()

    _for_rows(TM_DISP, issue)
    _for_rows(TM_DISP, drain)


def _dispatch(dest_flat, xn):
    return pl.pallas_call(
        _dispatch_kernel,
        grid_spec=pltpu.PrefetchScalarGridSpec(
            num_scalar_prefetch=1,
            grid=(N_TOK // TM_DISP,),
            in_specs=[pl.BlockSpec((TM_DISP, D_MODEL), lambda i, d: (i, 0))],
            out_specs=pl.BlockSpec(memory_space=pl.ANY),
            scratch_shapes=[pltpu.SemaphoreType.DMA(())],
        ),
        out_shape=jax.ShapeDtypeStruct((N_ROWS, D_MODEL), F32),
        compiler_params=pltpu.CompilerParams(dimension_semantics=("arbitrary",)),
        name="moe_dispatch",
    )(dest_flat, xn)


def _gmm_kernel(blk_ref, exp_ref, lo_ref, hi_ref, first_ref, newe_ref, xs_ref, wgu_ref, bgu_ref, wd_ref, bd_ref,
                o_ref, wgu_s, wd_s):
    w = pl.program_id(0)
    lo = lo_ref[w]
    hi = hi_ref[w]

    @pl.when(newe_ref[w] == 1)
    def _():
        wgu_s[...] = wgu_ref[0].astype(BF16)
        wd_s[...] = wd_ref[0].astype(BF16)

    @pl.when(hi > lo)
    def _():
        xb = xs_ref[...].astype(BF16)
        gate = _dot(xb, wgu_s[:, 0:D_EXPERT]) + bgu_ref[0, :, 0:D_EXPERT]
        up = _dot(xb, wgu_s[:, D_EXPERT:]) + bgu_ref[0, :, D_EXPERT:]
        gate = jnp.minimum(gate, SWIGLU_LIMIT)
        up = jnp.clip(up, -SWIGLU_LIMIT, SWIGLU_LIMIT)
        hid = (up + 1.0) * gate * jax.nn.sigmoid(SWIGLU_ALPHA * gate)
        y = _dot(hid.astype(BF16), wd_s[...]) + bd_ref[0]
        row = lax.broadcasted_iota(I32, (TM_GMM, 1), 0)
        mine = (row >= lo) & (row < hi)

        @pl.when(first_ref[w] == 1)
        def _():
            o_ref[...] = jnp.where(mine, y, 0.0)

        @pl.when(first_ref[w] == 0)
        def _():
            o_ref[...] = jnp.where(mine, y, o_ref[...])


def _gmm(item_blk, item_exp, item_lo, item_hi, item_first, item_newe, xs, w_gu, b_gu, w_down, b_down):
    rows = lambda w, blk, e, lo, hi, f, ne: (blk[w], 0)
    per_e = lambda w, blk, e, lo, hi, f, ne: (e[w], 0, 0)
    return pl.pallas_call(
        _gmm_kernel,
        grid_spec=pltpu.PrefetchScalarGridSpec(
            num_scalar_prefetch=6,
            grid=(N_ITEMS,),
            in_specs=[
                pl.BlockSpec((TM_GMM, D_MODEL), rows),
                pl.BlockSpec((1, D_MODEL, 2 * D_EXPERT), per_e),
                pl.BlockSpec((1, 1, 2 * D_EXPERT), per_e),
                pl.BlockSpec((1, D_EXPERT, D_MODEL), per_e),
                pl.BlockSpec((1, 1, D_MODEL), per_e),
            ],
            out_specs=pl.BlockSpec((TM_GMM, D_MODEL), rows),
            scratch_shapes=[pltpu.VMEM((D_MODEL, 2 * D_EXPERT), BF16), pltpu.VMEM((D_EXPERT, D_MODEL), BF16)],
        ),
        out_shape=jax.ShapeDtypeStruct((N_ROWS, D_MODEL), F32),
        compiler_params=pltpu.CompilerParams(
            dimension_semantics=("arbitrary",), vmem_limit_bytes=56 * 1024 * 1024),
        name="moe_experts",
    )(item_blk, item_exp, item_lo, item_hi, item_first, item_newe, xs, w_gu, b_gu, w_down, b_down)


def _combine_kernel(dest_ref, ys_ref, x2_ref, gate_ref, g_ref, o_ref, buf_ref, sem):
    t0 = pl.program_id(0) * TM_COMB

    def copies(r):
        return [_row_copy(ys_ref, dest_ref[(t0 + r) * TOP_K + k], buf_ref.at[k], r, sem) for k in range(TOP_K)]

    def issue(r):
        for cp in copies(r):
            cp.start()

    def drain(r):
        for cp in copies(r):
            cp.wait()

    _for_rows(TM_COMB, issue)
    _for_rows(TM_COMB, drain)
    gates = gate_ref[...]
    y = x2_ref[...]
    for k in range(TOP_K):
        y = y + buf_ref[k] * gates[:, k:k + 1]
    o_ref[...] = _rms(y, g_ref[...])


def _combine(dest_flat, ys, x2, gates_t, g):
    tok = lambda i, d: (i, 0)
    return pl.pallas_call(
        _combine_kernel,
        grid_spec=pltpu.PrefetchScalarGridSpec(
            num_scalar_prefetch=1,
            grid=(N_TOK // TM_COMB,),
            in_specs=[
                pl.BlockSpec(memory_space=pl.ANY),
                pl.BlockSpec((TM_COMB, D_MODEL), tok),
                pl.BlockSpec((TM_COMB, TOP_K), tok),
                pl.BlockSpec((1, D_MODEL), lambda i, d: (0, 0)),
            ],
            out_specs=pl.BlockSpec((TM_COMB, D_MODEL), tok),
            scratch_shapes=[pltpu.VMEM((TOP_K, TM_COMB, D_MODEL), F32), pltpu.SemaphoreType.DMA(())],
        ),
        out_shape=jax.ShapeDtypeStruct((N_TOK, D_MODEL), F32),
        compiler_params=pltpu.CompilerParams(
            dimension_semantics=("arbitrary",), vmem_limit_bytes=48 * 1024 * 1024),
        name="moe_combine",
    )(dest_flat, ys, x2, gates_t, g)


def _rotary_lane_table():
    half = ROT_DIM // 2
    inv_freq = ROPE_THETA ** (-jnp.arange(0, ROT_DIM, 2, dtype=F32) / ROT_DIM)
    rest = jnp.zeros((DIFF_HEAD_DIM - ROT_DIM,), F32)
    zh = jnp.zeros((half,), F32)
    oh = jnp.ones((half,), F32)
    reps = LANES // DIFF_HEAD_DIM
    rows = [jnp.tile(jnp.concatenate(r), reps) for r in
            ([inv_freq, inv_freq, rest], [-oh, zh, rest], [zh, oh, rest])]
    return jnp.concatenate([jnp.stack(rows), jnp.zeros((8 - len(rows), LANES), F32)])


def _work_items(counts):
    ends = jnp.cumsum(counts)
    starts = ends - counts
    first_blk = starts // TM_GMM
    last_blk = jnp.maximum(ends - 1, 0) // TM_GMM
    n_items = jnp.where(counts > 0, last_blk - first_blk + 1, 0)
    item_end = jnp.cumsum(n_items)
    item_start = item_end - n_items
    total = item_end[-1]
    w = jnp.arange(N_ITEMS, dtype=I32)
    valid = w < total
    wc = jnp.minimum(w, total - 1)
    e = jnp.minimum(jnp.sum((item_end[None, :] <= wc[:, None]).astype(I32), axis=1), N_EXPERTS - 1)
    blk = jnp.where(valid, first_blk[e] + (w - item_start[e]), N_BLOCKS - 1).astype(I32)
    lo = jnp.where(valid, jnp.maximum(starts[e] - blk * TM_GMM, 0), 0).astype(I32)
    hi = jnp.where(valid, jnp.minimum(ends[e] - blk * TM_GMM, TM_GMM), 0).astype(I32)
    first = (blk != jnp.concatenate([jnp.full((1,), -1, I32), blk[:-1]])).astype(I32)
    new_e = (valid & (e != jnp.concatenate([jnp.full((1,), -1, I32), e[:-1]]))).astype(I32)
    return blk, e, lo, hi, first, new_e, starts


def kernel(x, positions, mem, attn_norm_g, w_in, w_pool, pool_scale, lambda_q1, lambda_k1, lambda_q2, lambda_k2, subln_g, w_out, xattn_norm_g, mem_norm_g, w_cq, w_ckv, w_co, ffn_norm_g, w_router, b_router, w_gu, b_gu, w_down, b_down, final_norm_g):
    l = 0
    x2d = x.reshape(N_TOK, D_MODEL)
    u_pool, q, k, v = _in_proj(x2d, attn_norm_g[l].reshape(1, D_MODEL), w_in[l].astype(BF16),
                               positions.reshape(N_TOK, 1), _rotary_lane_table())
    y_pool = _pool(u_pool.reshape(BATCH, SEQ, POOL_WIDTH), w_pool[l].astype(BF16),
                   pool_scale[l].reshape(1, POOL_WIDTH))
    lam_vecs = jnp.stack([lambda_q1[l], lambda_k1[l], lambda_q2[l], lambda_k2[l]]).astype(F32)
    y_diff = _diff_attn(lam_vecs, q.reshape(BATCH, SEQ, QK_WIDTH), k.reshape(BATCH, SEQ, QK_WIDTH),
                        v.reshape(BATCH, SEQ, DIFF_WIDTH), subln_g[l].reshape(1, DIFF_V_DIM))
    kx, vx = _mem_kv(mem, mem_norm_g[l].reshape(1, D_MODEL), w_ckv[l].astype(BF16))

    wr_t = w_router[l].T
    wr_hi = wr_t.astype(BF16)
    wr_lo = (wr_t - wr_hi.astype(F32)).astype(BF16)
    tri = (jnp.arange(TM_MID)[:, None] < jnp.arange(TM_MID)[None, :]).astype(BF16)
    x2, xn, top_idx, gates, rank, cnt = _mid(
        x2d, y_pool.reshape(N_TOK, POOL_WIDTH), y_diff.reshape(N_TOK, DIFF_WIDTH), w_out[l].astype(BF16),
        xattn_norm_g[l].reshape(1, D_MODEL), w_cq[l].astype(BF16), kx, vx, w_co[l].astype(BF16),
        ffn_norm_g[l].reshape(1, D_MODEL), wr_hi, wr_lo, b_router[l].reshape(N_EXPERTS, 1), tri)

    counts = cnt[:, 0].astype(I32)
    item_blk, item_exp, item_lo, item_hi, item_first, item_newe, starts = _work_items(counts)
    hot = top_idx[:, :, None] == jnp.arange(N_EXPERTS, dtype=I32)
    dest = jnp.sum(jnp.where(hot, starts, 0), axis=-1) + rank
    dest_flat = dest.T.reshape(N_ROWS)

    xs = _dispatch(dest_flat, xn)
    ys = _gmm(item_blk, item_exp, item_lo, item_hi, item_first, item_newe, xs,
              w_gu[l], b_gu[l].reshape(N_EXPERTS, 1, 2 * D_EXPERT),
              w_down[l], b_down[l].reshape(N_EXPERTS, 1, D_MODEL))
    out = _combine(dest_flat, ys, x2, gates.T, final_norm_g.reshape(1, D_MODEL))
    return out.reshape(BATCH, SEQ, D_MODEL)
```

```python
import functools
import math

import jax
import jax.numpy as jnp
from jax import lax
from jax.experimental import pallas as pl
from jax.experimental.pallas import tpu as pltpu

F32 = jnp.float32
BF16 = jnp.bfloat16
I32 = jnp.int32

D_MODEL = 1024
BATCH = 8
SEQ = 2048
N_TOK = BATCH * SEQ
CHUNK = 64
NORM_EPS = 1e-5
POOL_WIDTH = 512
POOL_WINDOWS = (2, 4, 8, 16)
POOL_GROUP_DIM = 128
MAX_WINDOW = max(POOL_WINDOWS)
DIFF_HEADS = 4
DIFF_HEAD_DIM = 64
DIFF_V_DIM = 128
DIFF_WIDTH = 512
QK_WIDTH = 512
IN_WIDTH = 2048
ROT_DIM = 16
ROPE_THETA = 500000.0
MEM_LEN = 256
X_HEADS = 4
X_HEAD_DIM = 256
N_EXPERTS = 32
TOP_K = 4
D_EXPERT = 1024
SWIGLU_ALPHA = 1.702
SWIGLU_LIMIT = 7.0
LAM_INIT = 0.8 - 0.6 * math.exp(-0.3 * 0)
N_ROWS = N_TOK * TOP_K

LANES = 128

TM_IN = 512
TQ = 256
POOL_ROWS = 512
TM_MID = 512
TM_DISP = 256
TM_GMM = 512
GMM_ROWS = 256
TM_COMB = 256
ROW_UNROLL = 8
ROW_TILE = D_MODEL // LANES
N_BLOCKS = N_ROWS // TM_GMM
N_ITEMS = N_BLOCKS + N_EXPERTS - 1


def _rms(xf, g):
    ms = jnp.mean(xf * xf, axis=-1, keepdims=True)
    return xf * lax.rsqrt(ms + NORM_EPS) * g


def _dot(a, b):
    return jnp.dot(a, b, preferred_element_type=F32)


def _dot_nt(a, b):
    return lax.dot_general(a, b, (((1,), (1,)), ((), ())), preferred_element_type=F32)


def _in_proj_kernel(x_ref, g_ref, w_ref, pos_ref, rot_ref, up_ref, q_ref, k_ref, v_ref):
    h = _rms(x_ref[...], g_ref[...]).astype(BF16)
    ang = pos_ref[...].astype(F32) * rot_ref[0:1, :]
    cosf = jnp.cos(ang)
    sinf = jnp.sin(ang)
    sa = sinf * rot_ref[1:2, :]
    sb = sinf * rot_ref[2:3, :]
    up_ref[...] = _dot(h, w_ref[:, 0:POOL_WIDTH])
    for off, o_ref, scale in ((POOL_WIDTH, q_ref, DIFF_HEAD_DIM ** -0.5),
                              (POOL_WIDTH + QK_WIDTH, k_ref, 1.0)):
        u = _dot(h, w_ref[:, off:off + QK_WIDTH])
        for hd in range(DIFF_HEADS):
            uh = u[:, hd * LANES:(hd + 1) * LANES]
            r = uh * cosf + pltpu.roll(uh, LANES - 8, 1) * sa + pltpu.roll(uh, 8, 1) * sb
            o_ref[:, hd * LANES:(hd + 1) * LANES] = (r * scale).astype(BF16)
    v_ref[...] = _dot(h, w_ref[:, POOL_WIDTH + 2 * QK_WIDTH:IN_WIDTH]).astype(BF16)


def _in_proj(x2d, g, w_in, pos, rot_tab):
    n = N_TOK // TM_IN
    tok = lambda i: (i, 0)
    fixed = lambda i: (0, 0)
    return pl.pallas_call(
        _in_proj_kernel,
        grid=(n,),
        in_specs=[
            pl.BlockSpec((TM_IN, D_MODEL), tok),
            pl.BlockSpec((1, D_MODEL), fixed),
            pl.BlockSpec((D_MODEL, IN_WIDTH), fixed),
            pl.BlockSpec((TM_IN, 1), tok),
            pl.BlockSpec((8, LANES), fixed),
        ],
        out_specs=[
            pl.BlockSpec((TM_IN, POOL_WIDTH), tok),
            pl.BlockSpec((TM_IN, QK_WIDTH), tok),
            pl.BlockSpec((TM_IN, QK_WIDTH), tok),
            pl.BlockSpec((TM_IN, DIFF_WIDTH), tok),
        ],
        out_shape=[
            jax.ShapeDtypeStruct((N_TOK, POOL_WIDTH), F32),
            jax.ShapeDtypeStruct((N_TOK, QK_WIDTH), BF16),
            jax.ShapeDtypeStruct((N_TOK, QK_WIDTH), BF16),
            jax.ShapeDtypeStruct((N_TOK, DIFF_WIDTH), BF16),
        ],
        compiler_params=pltpu.CompilerParams(
            dimension_semantics=("arbitrary",), vmem_limit_bytes=48 * 1024 * 1024),
        name="in_proj",
    )(x2d, g, w_in, pos, rot_tab)


def _pool_kernel(u_ref, w_ref, sc_ref, o_ref, pad_ref):
    pad_ref[0:MAX_WINDOW, :] = jnp.zeros((MAX_WINDOW, POOL_WIDTH), F32)
    pad_ref[MAX_WINDOW:, :] = u_ref[0]
    for g, win in enumerate(POOL_WINDOWS):
        lanes = slice(g * POOL_GROUP_DIM, (g + 1) * POOL_GROUP_DIM)
        for c in range(SEQ // POOL_ROWS):
            r0 = c * POOL_ROWS
            u = pad_ref[MAX_WINDOW + r0:MAX_WINDOW + r0 + POOL_ROWS, lanes]
            acc = u
            for j in range(1, win):
                acc = acc + pad_ref[MAX_WINDOW + r0 - j:MAX_WINDOW + r0 - j + POOL_ROWS, lanes]
            t = r0 + lax.broadcasted_iota(I32, (POOL_ROWS, 1), 0)
            cnt = jnp.minimum(t + 1, win).astype(F32)
            mixed = (acc / cnt - u).astype(BF16)
            y = _dot(mixed, w_ref[g]) * sc_ref[:, lanes]
            o_ref[0, r0:r0 + POOL_ROWS, lanes] = y.astype(BF16)


def _pool(u_pool, w_pool, pool_scale):
    return pl.pallas_call(
        _pool_kernel,
        grid=(BATCH,),
        in_specs=[
            pl.BlockSpec((1, SEQ, POOL_WIDTH), lambda b: (b, 0, 0)),
            pl.BlockSpec((len(POOL_WINDOWS), POOL_GROUP_DIM, POOL_GROUP_DIM), lambda b: (0, 0, 0)),
            pl.BlockSpec((1, POOL_WIDTH), lambda b: (0, 0)),
        ],
        out_specs=pl.BlockSpec((1, SEQ, POOL_WIDTH), lambda b: (b, 0, 0)),
        out_shape=jax.ShapeDtypeStruct((BATCH, SEQ, POOL_WIDTH), BF16),
        scratch_shapes=[pltpu.VMEM((SEQ + MAX_WINDOW, POOL_WIDTH), F32)],
        compiler_params=pltpu.CompilerParams(
            dimension_semantics=("arbitrary",), vmem_limit_bytes=48 * 1024 * 1024),
        name="pool_mixer",
    )(u_pool, w_pool, pool_scale)


def _diff_attn_kernel(lam_ref, q_ref, k_ref, v_ref, g_ref, o_ref):
    lv = lam_ref[...]
    e1 = jnp.exp(jnp.sum(lv[0:1] * lv[1:2], axis=-1, keepdims=True))
    e2 = jnp.exp(jnp.sum(lv[2:3] * lv[3:4], axis=-1, keepdims=True))
    lam = e1 - e2 + LAM_INIT
    lane = lax.broadcasted_iota(I32, (TQ, LANES), 1)
    qc = lax.broadcasted_iota(I32, (TQ, TQ), 0) // CHUNK
    kc = lax.broadcasted_iota(I32, (TQ, TQ), 1) // CHUNK
    diag_mask = kc <= qc
    gain = g_ref[...] * (1.0 - LAM_INIT)
    zero = jnp.zeros((), BF16)
    for qi in range(SEQ // TQ):
        q0 = qi * TQ
        qt = q_ref[0, q0:q0 + TQ, :]
        qmaps = (jnp.where(lane < DIFF_HEAD_DIM, qt, zero), jnp.where(lane >= DIFF_HEAD_DIM, qt, zero))
        kd = k_ref[0, q0:q0 + TQ, :]
        s_diag = [jnp.where(diag_mask, _dot_nt(qm, kd), -jnp.inf) for qm in qmaps]
        if qi > 0:
            ka = k_ref[0, 0:q0, :]
            s_past = [_dot_nt(qm, ka) for qm in qmaps]
        p_diag, p_past, inv = [], [], []
        for m in range(2):
            mx = jnp.max(s_diag[m], axis=-1, keepdims=True)
            if qi > 0:
                mx = jnp.maximum(mx, jnp.max(s_past[m], axis=-1, keepdims=True))
            pd = jnp.exp(s_diag[m] - mx)
            den = jnp.sum(pd, axis=-1, keepdims=True)
            p_diag.append(pd)
            if qi > 0:
                pp = jnp.exp(s_past[m] - mx)
                den = den + jnp.sum(pp, axis=-1, keepdims=True)
                p_past.append(pp)
            inv.append(1.0 / den)
        c1 = inv[0]
        c2 = lam * inv[1]
        a_diag = (p_diag[0] * c1 - p_diag[1] * c2).astype(BF16)
        o = _dot(a_diag, v_ref[0, q0:q0 + TQ, :])
        if qi > 0:
            a_past = (p_past[0] * c1 - p_past[1] * c2).astype(BF16)
            o = o + _dot(a_past, v_ref[0, 0:q0, :])
        o_ref[0, q0:q0 + TQ, :] = _rms(o, gain).astype(BF16)


def _diff_attn(lam_vecs, q, k, v, subln_g):
    blk = pl.BlockSpec((1, SEQ, LANES), lambda b, h: (b, 0, h))
    return pl.pallas_call(
        _diff_attn_kernel,
        grid=(BATCH, DIFF_HEADS),
        in_specs=[
            pl.BlockSpec((4, DIFF_HEAD_DIM), lambda b, h: (0, 0)),
            blk, blk, blk,
            pl.BlockSpec((1, DIFF_V_DIM), lambda b, h: (0, 0)),
        ],
        out_specs=blk,
        out_shape=jax.ShapeDtypeStruct((BATCH, SEQ, DIFF_WIDTH), BF16),
        compiler_params=pltpu.CompilerParams(
            dimension_semantics=("arbitrary", "arbitrary"), vmem_limit_bytes=48 * 1024 * 1024),
        name="diff_attn",
    )(lam_vecs, q, k, v, subln_g)


def _mem_kv_kernel(m_ref, g_ref, w_ref, k_ref, v_ref):
    h = _rms(m_ref[0], g_ref[...]).astype(BF16)
    k_ref[0] = _dot(h, w_ref[:, 0:D_MODEL]).astype(BF16)
    v_ref[0] = _dot(h, w_ref[:, D_MODEL:2 * D_MODEL]).astype(BF16)


def _mem_kv(mem, g, w_ckv):
    blk = pl.BlockSpec((1, MEM_LEN, D_MODEL), lambda b: (b, 0, 0))
    return pl.pallas_call(
        _mem_kv_kernel,
        grid=(BATCH,),
        in_specs=[blk, pl.BlockSpec((1, D_MODEL), lambda b: (0, 0)),
                  pl.BlockSpec((D_MODEL, 2 * D_MODEL), lambda b: (0, 0))],
        out_specs=[blk, blk],
        out_shape=[jax.ShapeDtypeStruct((BATCH, MEM_LEN, D_MODEL), BF16)] * 2,
        compiler_params=pltpu.CompilerParams(
            dimension_semantics=("arbitrary",), vmem_limit_bytes=48 * 1024 * 1024),
        name="mem_kv",
    )(mem, g, w_ckv)


def _mid_kernel(x_ref, yp_ref, yd_ref, wo_ref, g2_ref, wcq_ref, kx_ref, vx_ref, wco_ref, g3_ref,
                wrh_ref, wrl_ref, br_ref, tri_ref,
                x2_ref, xn_ref, idx_ref, gate_ref, rank_ref, cnt_ref, attn_ref, base_ref):
    @pl.when(pl.program_id(0) == 0)
    def _():
        base_ref[...] = jnp.zeros_like(base_ref)

    x1 = x_ref[...] + _dot(yp_ref[...], wo_ref[0:POOL_WIDTH, :]) + _dot(yd_ref[...], wo_ref[POOL_WIDTH:, :])
    h2 = _rms(x1, g2_ref[...]).astype(BF16)
    qx = (_dot(h2, wcq_ref[...]) * (X_HEAD_DIM ** -0.5)).astype(BF16)
    for hd in range(X_HEADS):
        cols = slice(hd * X_HEAD_DIM, (hd + 1) * X_HEAD_DIM)
        s = _dot_nt(qx[:, cols], kx_ref[0, :, cols])
        p = jnp.exp(s - jnp.max(s, axis=-1, keepdims=True))
        p = p / jnp.sum(p, axis=-1, keepdims=True)
        attn_ref[:, cols] = _dot(p.astype(BF16), vx_ref[0, :, cols]).astype(BF16)
    x2 = x1 + _dot(attn_ref[...], wco_ref[...])
    x2_ref[...] = x2
    xn = _rms(x2, g3_ref[...])
    xn_ref[...] = xn

    xh = xn.astype(BF16)
    xl = (xn - xh.astype(F32)).astype(BF16)
    logits = (_dot_nt(wrh_ref[...], xh) + _dot_nt(wrh_ref[...], xl) + _dot_nt(wrl_ref[...], xh)
              + br_ref[...])
    eidx = lax.broadcasted_iota(I32, (N_EXPERTS, TM_MID), 0).astype(F32)
    vals, hots = [], []
    for k in range(TOP_K):
        mx = jnp.max(logits, axis=0, keepdims=True)
        sel = jnp.min(jnp.where(logits == mx, eidx, float(N_EXPERTS)), axis=0, keepdims=True)
        hot = eidx == sel
        idx_ref[k:k + 1, :] = sel.astype(I32)
        vals.append(mx)
        hots.append(hot)
        logits = jnp.where(hot, -jnp.inf, logits)
    ex = [jnp.exp(v - vals[0]) for v in vals]
    den = ex[0] + ex[1] + ex[2] + ex[3]
    for k in range(TOP_K):
        gate_ref[k:k + 1, :] = ex[k] / den

    chosen = jnp.zeros((N_EXPERTS, TM_MID), F32)
    for k in range(TOP_K):
        chosen = chosen + jnp.where(hots[k], 1.0, 0.0)
    before = _dot(chosen.astype(BF16), tri_ref[...]) + base_ref[:, 0:1]
    for k in range(TOP_K):
        rank_ref[k:k + 1, :] = jnp.sum(jnp.where(hots[k], before, 0.0), axis=0, keepdims=True).astype(I32)
    base_ref[...] = base_ref[...] + jnp.sum(chosen, axis=1, keepdims=True)
    cnt_ref[...] = base_ref[...]


def _mid(x2d, y_pool, y_diff, w_out, g2, w_cq, kx, vx, w_co, g3, wr_hi, wr_lo, b_r, tri):
    n = N_TOK // TM_MID
    per_batch = SEQ // TM_MID
    tok = lambda i: (i, 0)
    fixed = lambda i: (0, 0)
    kv = pl.BlockSpec((1, MEM_LEN, D_MODEL), lambda i: (i // per_batch, 0, 0))
    row4 = pl.BlockSpec((TOP_K, TM_MID), lambda i: (0, i))
    return pl.pallas_call(
        _mid_kernel,
        grid=(n,),
        in_specs=[
            pl.BlockSpec((TM_MID, D_MODEL), tok),
            pl.BlockSpec((TM_MID, POOL_WIDTH), tok),
            pl.BlockSpec((TM_MID, DIFF_WIDTH), tok),
            pl.BlockSpec((D_MODEL, D_MODEL), fixed),
            pl.BlockSpec((1, D_MODEL), fixed),
            pl.BlockSpec((D_MODEL, D_MODEL), fixed),
            kv, kv,
            pl.BlockSpec((D_MODEL, D_MODEL), fixed),
            pl.BlockSpec((1, D_MODEL), fixed),
            pl.BlockSpec((N_EXPERTS, D_MODEL), fixed),
            pl.BlockSpec((N_EXPERTS, D_MODEL), fixed),
            pl.BlockSpec((N_EXPERTS, 1), fixed),
            pl.BlockSpec((TM_MID, TM_MID), fixed),
        ],
        out_specs=[
            pl.BlockSpec((TM_MID, D_MODEL), tok),
            pl.BlockSpec((TM_MID, D_MODEL), tok),
            row4, row4, row4,
            pl.BlockSpec((N_EXPERTS, LANES), fixed),
        ],
        out_shape=[
            jax.ShapeDtypeStruct((N_TOK, D_MODEL), F32),
            jax.ShapeDtypeStruct((N_TOK, D_MODEL), F32),
            jax.ShapeDtypeStruct((TOP_K, N_TOK), I32),
            jax.ShapeDtypeStruct((TOP_K, N_TOK), F32),
            jax.ShapeDtypeStruct((TOP_K, N_TOK), I32),
            jax.ShapeDtypeStruct((N_EXPERTS, LANES), F32),
        ],
        scratch_shapes=[pltpu.VMEM((TM_MID, D_MODEL), BF16), pltpu.VMEM((N_EXPERTS, LANES), F32)],
        compiler_params=pltpu.CompilerParams(
            dimension_semantics=("arbitrary",), vmem_limit_bytes=56 * 1024 * 1024),
        name="mid_block",
    )(x2d, y_pool, y_diff, w_out, g2, w_cq, kx, vx, w_co, g3, wr_hi, wr_lo, b_r, tri)


def _for_rows(n_rows, fn):
    def body(g, c):
        for j in range(ROW_UNROLL):
            fn(g * ROW_UNROLL + j)
        return c

    lax.fori_loop(0, n_rows // ROW_UNROLL, body, 0)


def _to_row_tiles(dst_ref, val, row0=0):
    n = val.shape[0]
    for s in range(ROW_TILE):
        dst_ref[pl.ds(row0 * ROW_TILE + s, n, stride=ROW_TILE), :] = val[:, s * LANES:(s + 1) * LANES]


def _from_row_tiles(src_ref, n, row0=0):
    return jnp.concatenate(
        [src_ref[pl.ds(row0 * ROW_TILE + s, n, stride=ROW_TILE), :] for s in range(ROW_TILE)], axis=1)


def _tile_rows(row):
    return pl.ds(pl.multiple_of(row * ROW_TILE, ROW_TILE), ROW_TILE)


def _dispatch_kernel(dest_ref, xn_ref, xs_ref, stage_ref, sems):
    i = pl.program_id(0)
    last = pl.num_programs(0) - 1

    def copies(step, slot, r):
        return [pltpu.make_async_copy(stage_ref.at[slot, _tile_rows(r)],
                                      xs_ref.at[_tile_rows(dest_ref[(step * TM_DISP + r) * TOP_K + k])],
                                      sems.at[slot]) for k in range(TOP_K)]

    def start(step, slot):
        def fn(r):
            for k, cp in enumerate(copies(step, slot, r)):
                cp.start(priority=k % 2)
        _for_rows(TM_DISP, fn)

    def wait(step, slot):
        def fn(r):
            for cp in copies(step, slot, r):
                cp.wait()
        _for_rows(TM_DISP, fn)

    for slot in range(2):
        @pl.when(i % 2 == slot)
        def _():
            _to_row_tiles(stage_ref.at[slot], xn_ref[...])
            start(i, slot)

            @pl.when(i > 0)
            def _():
                wait(i - 1, 1 - slot)

            @pl.when(i == last)
            def _():
                wait(i, slot)


def _dispatch(dest_flat, xn):
    return pl.pallas_call(
        _dispatch_kernel,
        grid_spec=pltpu.PrefetchScalarGridSpec(
            num_scalar_prefetch=1,
            grid=(N_TOK // TM_DISP,),
            in_specs=[pl.BlockSpec((TM_DISP, D_MODEL), lambda i, d: (i, 0))],
            out_specs=pl.BlockSpec(memory_space=pl.ANY),
            scratch_shapes=[pltpu.VMEM((2, TM_DISP * ROW_TILE, LANES), F32), pltpu.SemaphoreType.DMA((2,))],
        ),
        out_shape=jax.ShapeDtypeStruct((N_ROWS * ROW_TILE, LANES), F32),
        compiler_params=pltpu.CompilerParams(dimension_semantics=("arbitrary",)),
        name="moe_dispatch",
    )(dest_flat, xn)


def _gmm_kernel(blk_ref, exp_ref, lo_ref, hi_ref, first_ref, newe_ref, xs_ref, wgu_ref, bgu_ref, wd_ref, bd_ref,
                o_ref, wgu_s, wd_s):
    w = pl.program_id(0)
    lo = lo_ref[w]
    hi = hi_ref[w]

    @pl.when(newe_ref[w] == 1)
    def _():
        wgu_s[...] = wgu_ref[0].astype(BF16)
        wd_s[...] = wd_ref[0].astype(BF16)

    def expert_ffn(merge):
        for r0 in range(0, TM_GMM, GMM_ROWS):
            xb = _from_row_tiles(xs_ref, GMM_ROWS, r0).astype(BF16)
            gate = _dot(xb, wgu_s[:, 0:D_EXPERT]) + bgu_ref[0, :, 0:D_EXPERT]
            up = _dot(xb, wgu_s[:, D_EXPERT:]) + bgu_ref[0, :, D_EXPERT:]
            gate = jnp.minimum(gate, SWIGLU_LIMIT)
            up = jnp.clip(up, -SWIGLU_LIMIT, SWIGLU_LIMIT)
            hid = (up + 1.0) * gate * jax.nn.sigmoid(SWIGLU_ALPHA * gate)
            y = _dot(hid.astype(BF16), wd_s[...]) + bd_ref[0]
            row = r0 + lax.broadcasted_iota(I32, (GMM_ROWS, 1), 0)
            mine = (row >= lo) & (row < hi)
            other = _from_row_tiles(o_ref, GMM_ROWS, r0) if merge else 0.0
            _to_row_tiles(o_ref, jnp.where(mine, y, other), r0)

    @pl.when((hi > lo) & (first_ref[w] == 1))
    def _():
        expert_ffn(merge=False)

    @pl.when((hi > lo) & (first_ref[w] == 0))
    def _():
        expert_ffn(merge=True)


def _gmm(item_blk, item_exp, item_lo, item_hi, item_first, item_newe, xs, w_gu, b_gu, w_down, b_down):
    rows = lambda w, blk, e, lo, hi, f, ne: (blk[w], 0)
    per_e = lambda w, blk, e, lo, hi, f, ne: (e[w], 0, 0)
    return pl.pallas_call(
        _gmm_kernel,
        grid_spec=pltpu.PrefetchScalarGridSpec(
            num_scalar_prefetch=6,
            grid=(N_ITEMS,),
            in_specs=[
                pl.BlockSpec((TM_GMM * ROW_TILE, LANES), rows),
                pl.BlockSpec((1, D_MODEL, 2 * D_EXPERT), per_e),
                pl.BlockSpec((1, 1, 2 * D_EXPERT), per_e),
                pl.BlockSpec((1, D_EXPERT, D_MODEL), per_e),
                pl.BlockSpec((1, 1, D_MODEL), per_e),
            ],
            out_specs=pl.BlockSpec((TM_GMM * ROW_TILE, LANES), rows),
            scratch_shapes=[pltpu.VMEM((D_MODEL, 2 * D_EXPERT), BF16), pltpu.VMEM((D_EXPERT, D_MODEL), BF16)],
        ),
        out_shape=jax.ShapeDtypeStruct((N_ROWS * ROW_TILE, LANES), F32),
        compiler_params=pltpu.CompilerParams(
            dimension_semantics=("arbitrary",), vmem_limit_bytes=56 * 1024 * 1024),
        name="moe_experts",
    )(item_blk, item_exp, item_lo, item_hi, item_first, item_newe, xs, w_gu, b_gu, w_down, b_down)


def _combine_kernel(dest_ref, ys_ref, x2_ref, gate_ref, g_ref, o_ref, buf_ref, sems):
    i = pl.program_id(0)
    last = pl.num_programs(0) - 1

    def copies(step, slot, r):
        return [pltpu.make_async_copy(ys_ref.at[_tile_rows(dest_ref[(step * TM_COMB + r) * TOP_K + k])],
                                      buf_ref.at[slot, k, _tile_rows(r)],
                                      sems.at[slot]) for k in range(TOP_K)]

    def start(step, slot):
        def fn(r):
            for k, cp in enumerate(copies(step, slot, r)):
                cp.start(priority=k % 2)
        _for_rows(TM_COMB, fn)

    def wait(step, slot):
        def fn(r):
            for cp in copies(step, slot, r):
                cp.wait()
        _for_rows(TM_COMB, fn)

    @pl.when(i == 0)
    def _():
        start(0, 0)

    for slot in range(2):
        @pl.when(i % 2 == slot)
        def _():
            @pl.when(i < last)
            def _():
                start(i + 1, 1 - slot)

            wait(i, slot)
            gates = gate_ref[...]
            y = x2_ref[...]
            for k in range(TOP_K):
                y = y + _from_row_tiles(buf_ref.at[slot, k], TM_COMB) * gates[:, k:k + 1]
            o_ref[...] = _rms(y, g_ref[...])


def _combine(dest_flat, ys, x2, gates_t, g):
    tok = lambda i, d: (i, 0)
    return pl.pallas_call(
        _combine_kernel,
        grid_spec=pltpu.PrefetchScalarGridSpec(
            num_scalar_prefetch=1,
            grid=(N_TOK // TM_COMB,),
            in_specs=[
                pl.BlockSpec(memory_space=pl.ANY),
                pl.BlockSpec((TM_COMB, D_MODEL), tok),
                pl.BlockSpec((TM_COMB, TOP_K), tok),
                pl.BlockSpec((1, D_MODEL), lambda i, d: (0, 0)),
            ],
            out_specs=pl.BlockSpec((TM_COMB, D_MODEL), tok),
            scratch_shapes=[pltpu.VMEM((2, TOP_K, TM_COMB * ROW_TILE, LANES), F32),
                            pltpu.SemaphoreType.DMA((2,))],
        ),
        out_shape=jax.ShapeDtypeStruct((N_TOK, D_MODEL), F32),
        compiler_params=pltpu.CompilerParams(
            dimension_semantics=("arbitrary",), vmem_limit_bytes=48 * 1024 * 1024),
        name="moe_combine",
    )(dest_flat, ys, x2, gates_t, g)


def _rotary_lane_table():
    half = ROT_DIM // 2
    inv_freq = ROPE_THETA ** (-jnp.arange(0, ROT_DIM, 2, dtype=F32) / ROT_DIM)
    rest = jnp.zeros((DIFF_HEAD_DIM - ROT_DIM,), F32)
    zh = jnp.zeros((half,), F32)
    oh = jnp.ones((half,), F32)
    reps = LANES // DIFF_HEAD_DIM
    rows = [jnp.tile(jnp.concatenate(r), reps) for r in
            ([inv_freq, inv_freq, rest], [-oh, zh, rest], [zh, oh, rest])]
    return jnp.concatenate([jnp.stack(rows), jnp.zeros((8 - len(rows), LANES), F32)])


def _work_items(counts):
    ends = jnp.cumsum(counts)
    starts = ends - counts
    first_blk = starts // TM_GMM
    last_blk = jnp.maximum(ends - 1, 0) // TM_GMM
    n_items = jnp.where(counts > 0, last_blk - first_blk + 1, 0)
    item_end = jnp.cumsum(n_items)
    item_start = item_end - n_items
    total = item_end[-1]
    w = jnp.arange(N_ITEMS, dtype=I32)
    valid = w < total
    wc = jnp.minimum(w, total - 1)
    e = jnp.minimum(jnp.sum((item_end[None, :] <= wc[:, None]).astype(I32), axis=1), N_EXPERTS - 1)
    is_e = e[:, None] == jnp.arange(N_EXPERTS, dtype=I32)[None, :]
    pick = lambda per_expert: jnp.sum(jnp.where(is_e, per_expert[None, :], 0), axis=1)
    blk = jnp.where(valid, pick(first_blk) + (w - pick(item_start)), N_BLOCKS - 1).astype(I32)
    lo = jnp.where(valid, jnp.maximum(pick(starts) - blk * TM_GMM, 0), 0).astype(I32)
    hi = jnp.where(valid, jnp.minimum(pick(ends) - blk * TM_GMM, TM_GMM), 0).astype(I32)
    first = (blk != jnp.concatenate([jnp.full((1,), -1, I32), blk[:-1]])).astype(I32)
    new_e = (valid & (e != jnp.concatenate([jnp.full((1,), -1, I32), e[:-1]]))).astype(I32)
    return blk, e, lo, hi, first, new_e, starts


def kernel(x, positions, mem, attn_norm_g, w_in, w_pool, pool_scale, lambda_q1, lambda_k1, lambda_q2, lambda_k2, subln_g, w_out, xattn_norm_g, mem_norm_g, w_cq, w_ckv, w_co, ffn_norm_g, w_router, b_router, w_gu, b_gu, w_down, b_down, final_norm_g):
    l = 0
    x2d = x.reshape(N_TOK, D_MODEL)
    u_pool, q, k, v = _in_proj(x2d, attn_norm_g[l].reshape(1, D_MODEL), w_in[l].astype(BF16),
                               positions.reshape(N_TOK, 1), _rotary_lane_table())
    y_pool = _pool(u_pool.reshape(BATCH, SEQ, POOL_WIDTH), w_pool[l].astype(BF16),
                   pool_scale[l].reshape(1, POOL_WIDTH))
    lam_vecs = jnp.stack([lambda_q1[l], lambda_k1[l], lambda_q2[l], lambda_k2[l]]).astype(F32)
    y_diff = _diff_attn(lam_vecs, q.reshape(BATCH, SEQ, QK_WIDTH), k.reshape(BATCH, SEQ, QK_WIDTH),
                        v.reshape(BATCH, SEQ, DIFF_WIDTH), subln_g[l].reshape(1, DIFF_V_DIM))
    kx, vx = _mem_kv(mem, mem_norm_g[l].reshape(1, D_MODEL), w_ckv[l].astype(BF16))

    wr_t = w_router[l].T
    wr_hi = wr_t.astype(BF16)
    wr_lo = (wr_t - wr_hi.astype(F32)).astype(BF16)
    tri = (jnp.arange(TM_MID)[:, None] < jnp.arange(TM_MID)[None, :]).astype(BF16)
    x2, xn, top_idx, gates, rank, cnt = _mid(
        x2d, y_pool.reshape(N_TOK, POOL_WIDTH), y_diff.reshape(N_TOK, DIFF_WIDTH), w_out[l].astype(BF16),
        xattn_norm_g[l].reshape(1, D_MODEL), w_cq[l].astype(BF16), kx, vx, w_co[l].astype(BF16),
        ffn_norm_g[l].reshape(1, D_MODEL), wr_hi, wr_lo, b_router[l].reshape(N_EXPERTS, 1), tri)

    counts = cnt[:, 0].astype(I32)
    item_blk, item_exp, item_lo, item_hi, item_first, item_newe, starts = _work_items(counts)
    hot = top_idx[:, :, None] == jnp.arange(N_EXPERTS, dtype=I32)
    dest = jnp.sum(jnp.where(hot, starts, 0), axis=-1) + rank
    dest_flat = dest.T.reshape(N_ROWS)

    xs = _dispatch(dest_flat, xn)
    ys = _gmm(item_blk, item_exp, item_lo, item_hi, item_first, item_newe, xs,
              w_gu[l], b_gu[l].reshape(N_EXPERTS, 1, 2 * D_EXPERT),
              w_down[l], b_down[l].reshape(N_EXPERTS, 1, D_MODEL))
    out = _combine(dest_flat, ys, x2, gates.T, final_norm_g.reshape(1, D_MODEL))
    return out.reshape(BATCH, SEQ, D_MODEL)
```

```python
import functools
import math

import jax
import jax.numpy as jnp
from jax import lax
from jax.experimental import pallas as pl
from jax.experimental.pallas import tpu as pltpu

F32 = jnp.float32
BF16 = jnp.bfloat16
I32 = jnp.int32

D_MODEL = 1024
BATCH = 8
SEQ = 2048
N_TOK = BATCH * SEQ
CHUNK = 64
NORM_EPS = 1e-5
POOL_WIDTH = 512
POOL_WINDOWS = (2, 4, 8, 16)
POOL_GROUP_DIM = 128
MAX_WINDOW = max(POOL_WINDOWS)
DIFF_HEADS = 4
DIFF_HEAD_DIM = 64
DIFF_V_DIM = 128
DIFF_WIDTH = 512
QK_WIDTH = 512
IN_WIDTH = 2048
ROT_DIM = 16
ROPE_THETA = 500000.0
MEM_LEN = 256
X_HEADS = 4
X_HEAD_DIM = 256
N_EXPERTS = 32
TOP_K = 4
D_EXPERT = 1024
SWIGLU_ALPHA = 1.702
SWIGLU_LIMIT = 7.0
LAM_INIT = 0.8 - 0.6 * math.exp(-0.3 * 0)
N_ROWS = N_TOK * TOP_K

LANES = 128

TM_IN = 512
IN_ROWS = 512
TQ = 256
POOL_ROWS = 512
TM_MID = 512
MID_ROWS = 512
TM_DISP = 256
TM_GMM = 512
GMM_ROWS = 256
TM_COMB = 256
COMB_ROWS = 32
ROW_UNROLL = 8
ROW_TILE = D_MODEL // LANES
N_BLOCKS = N_ROWS // TM_GMM
N_ITEMS = N_BLOCKS + N_EXPERTS - 1


def _rms(xf, g):
    ms = jnp.mean(xf * xf, axis=-1, keepdims=True)
    return xf * lax.rsqrt(ms + NORM_EPS) * g


def _dot(a, b):
    return jnp.dot(a, b, preferred_element_type=F32)


def _dot_nt(a, b):
    return lax.dot_general(a, b, (((1,), (1,)), ((), ())), preferred_element_type=F32)


def _in_proj_kernel(x_ref, g_ref, w_ref, pos_ref, rot_ref, up_ref, q_ref, k_ref, v_ref):
    for r0 in range(0, TM_IN, IN_ROWS):
        rows = slice(r0, r0 + IN_ROWS)
        h = _rms(x_ref[rows, :], g_ref[...]).astype(BF16)
        ang = pos_ref[rows, :].astype(F32) * rot_ref[0:1, :]
        cosf = jnp.cos(ang)
        sinf = jnp.sin(ang)
        sa = sinf * rot_ref[1:2, :]
        sb = sinf * rot_ref[2:3, :]
        up_ref[rows, :] = _dot(h, w_ref[:, 0:POOL_WIDTH])
        for off, o_ref, scale in ((POOL_WIDTH, q_ref, DIFF_HEAD_DIM ** -0.5),
                                  (POOL_WIDTH + QK_WIDTH, k_ref, 1.0)):
            u = _dot(h, w_ref[:, off:off + QK_WIDTH])
            for hd in range(DIFF_HEADS):
                uh = u[:, hd * LANES:(hd + 1) * LANES]
                r = uh * cosf + pltpu.roll(uh, LANES - 8, 1) * sa + pltpu.roll(uh, 8, 1) * sb
                o_ref[rows, hd * LANES:(hd + 1) * LANES] = (r * scale).astype(BF16)
        v_ref[rows, :] = _dot(h, w_ref[:, POOL_WIDTH + 2 * QK_WIDTH:IN_WIDTH]).astype(BF16)


def _in_proj(x2d, g, w_in, pos, rot_tab):
    n = N_TOK // TM_IN
    tok = lambda i: (i, 0)
    fixed = lambda i: (0, 0)
    return pl.pallas_call(
        _in_proj_kernel,
        grid=(n,),
        in_specs=[
            pl.BlockSpec((TM_IN, D_MODEL), tok),
            pl.BlockSpec((1, D_MODEL), fixed),
            pl.BlockSpec((D_MODEL, IN_WIDTH), fixed),
            pl.BlockSpec((TM_IN, 1), tok),
            pl.BlockSpec((8, LANES), fixed),
        ],
        out_specs=[
            pl.BlockSpec((TM_IN, POOL_WIDTH), tok),
            pl.BlockSpec((TM_IN, QK_WIDTH), tok),
            pl.BlockSpec((TM_IN, QK_WIDTH), tok),
            pl.BlockSpec((TM_IN, DIFF_WIDTH), tok),
        ],
        out_shape=[
            jax.ShapeDtypeStruct((N_TOK, POOL_WIDTH), F32),
            jax.ShapeDtypeStruct((N_TOK, QK_WIDTH), BF16),
            jax.ShapeDtypeStruct((N_TOK, QK_WIDTH), BF16),
            jax.ShapeDtypeStruct((N_TOK, DIFF_WIDTH), BF16),
        ],
        compiler_params=pltpu.CompilerParams(
            dimension_semantics=("arbitrary",), vmem_limit_bytes=48 * 1024 * 1024),
        name="in_proj",
    )(x2d, g, w_in, pos, rot_tab)


def _pool_kernel(u_ref, w_ref, sc_ref, o_ref, pad_ref):
    pad_ref[0:MAX_WINDOW, :] = jnp.zeros((MAX_WINDOW, POOL_WIDTH), F32)
    pad_ref[MAX_WINDOW:, :] = u_ref[0]
    for g, win in enumerate(POOL_WINDOWS):
        lanes = slice(g * POOL_GROUP_DIM, (g + 1) * POOL_GROUP_DIM)
        for c in range(SEQ // POOL_ROWS):
            r0 = c * POOL_ROWS
            u = pad_ref[MAX_WINDOW + r0:MAX_WINDOW + r0 + POOL_ROWS, lanes]
            acc = u
            for j in range(1, win):
                acc = acc + pad_ref[MAX_WINDOW + r0 - j:MAX_WINDOW + r0 - j + POOL_ROWS, lanes]
            t = r0 + lax.broadcasted_iota(I32, (POOL_ROWS, 1), 0)
            cnt = jnp.minimum(t + 1, win).astype(F32)
            mixed = (acc / cnt - u).astype(BF16)
            y = _dot(mixed, w_ref[g]) * sc_ref[:, lanes]
            o_ref[0, r0:r0 + POOL_ROWS, lanes] = y.astype(BF16)


def _pool(u_pool, w_pool, pool_scale):
    return pl.pallas_call(
        _pool_kernel,
        grid=(BATCH,),
        in_specs=[
            pl.BlockSpec((1, SEQ, POOL_WIDTH), lambda b: (b, 0, 0)),
            pl.BlockSpec((len(POOL_WINDOWS), POOL_GROUP_DIM, POOL_GROUP_DIM), lambda b: (0, 0, 0)),
            pl.BlockSpec((1, POOL_WIDTH), lambda b: (0, 0)),
        ],
        out_specs=pl.BlockSpec((1, SEQ, POOL_WIDTH), lambda b: (b, 0, 0)),
        out_shape=jax.ShapeDtypeStruct((BATCH, SEQ, POOL_WIDTH), BF16),
        scratch_shapes=[pltpu.VMEM((SEQ + MAX_WINDOW, POOL_WIDTH), F32)],
        compiler_params=pltpu.CompilerParams(
            dimension_semantics=("arbitrary",), vmem_limit_bytes=48 * 1024 * 1024),
        name="pool_mixer",
    )(u_pool, w_pool, pool_scale)


def _diff_attn_kernel(lam_ref, q_ref, k_ref, v_ref, g_ref, o_ref):
    lv = lam_ref[...]
    e1 = jnp.exp(jnp.sum(lv[0:1] * lv[1:2], axis=-1, keepdims=True))
    e2 = jnp.exp(jnp.sum(lv[2:3] * lv[3:4], axis=-1, keepdims=True))
    lam = e1 - e2 + LAM_INIT
    lane = lax.broadcasted_iota(I32, (TQ, LANES), 1)
    qc = lax.broadcasted_iota(I32, (TQ, TQ), 0) // CHUNK
    kc = lax.broadcasted_iota(I32, (TQ, TQ), 1) // CHUNK
    diag_mask = kc <= qc
    gain = g_ref[...] * (1.0 - LAM_INIT)
    zero = jnp.zeros((), BF16)
    for qi in range(SEQ // TQ):
        q0 = qi * TQ
        qt = q_ref[0, q0:q0 + TQ, :]
        qmaps = (jnp.where(lane < DIFF_HEAD_DIM, qt, zero), jnp.where(lane >= DIFF_HEAD_DIM, qt, zero))
        kd = k_ref[0, q0:q0 + TQ, :]
        s_diag = [jnp.where(diag_mask, _dot_nt(qm, kd), -jnp.inf) for qm in qmaps]
        if qi > 0:
            ka = k_ref[0, 0:q0, :]
            s_past = [_dot_nt(qm, ka) for qm in qmaps]
        p_diag, p_past, inv = [], [], []
        for m in range(2):
            mx = jnp.max(s_diag[m], axis=-1, keepdims=True)
            if qi > 0:
                mx = jnp.maximum(mx, jnp.max(s_past[m], axis=-1, keepdims=True))
            pd = jnp.exp(s_diag[m] - mx)
            den = jnp.sum(pd, axis=-1, keepdims=True)
            p_diag.append(pd)
            if qi > 0:
                pp = jnp.exp(s_past[m] - mx)
                den = den + jnp.sum(pp, axis=-1, keepdims=True)
                p_past.append(pp)
            inv.append(1.0 / den)
        c1 = inv[0]
        c2 = lam * inv[1]
        a_diag = (p_diag[0] * c1 - p_diag[1] * c2).astype(BF16)
        o = _dot(a_diag, v_ref[0, q0:q0 + TQ, :])
        if qi > 0:
            a_past = (p_past[0] * c1 - p_past[1] * c2).astype(BF16)
            o = o + _dot(a_past, v_ref[0, 0:q0, :])
        o_ref[0, q0:q0 + TQ, :] = _rms(o, gain).astype(BF16)


def _diff_attn(lam_vecs, q, k, v, subln_g):
    blk = pl.BlockSpec((1, SEQ, LANES), lambda b, h: (b, 0, h))
    return pl.pallas_call(
        _diff_attn_kernel,
        grid=(BATCH, DIFF_HEADS),
        in_specs=[
            pl.BlockSpec((4, DIFF_HEAD_DIM), lambda b, h: (0, 0)),
            blk, blk, blk,
            pl.BlockSpec((1, DIFF_V_DIM), lambda b, h: (0, 0)),
        ],
        out_specs=blk,
        out_shape=jax.ShapeDtypeStruct((BATCH, SEQ, DIFF_WIDTH), BF16),
        compiler_params=pltpu.CompilerParams(
            dimension_semantics=("arbitrary", "arbitrary"), vmem_limit_bytes=48 * 1024 * 1024),
        name="diff_attn",
    )(lam_vecs, q, k, v, subln_g)


def _mem_kv_kernel(m_ref, g_ref, w_ref, k_ref, v_ref):
    h = _rms(m_ref[0], g_ref[...]).astype(BF16)
    k_ref[0] = _dot(h, w_ref[:, 0:D_MODEL]).astype(BF16)
    v_ref[0] = _dot(h, w_ref[:, D_MODEL:2 * D_MODEL]).astype(BF16)


def _mem_kv(mem, g, w_ckv):
    blk = pl.BlockSpec((1, MEM_LEN, D_MODEL), lambda b: (b, 0, 0))
    return pl.pallas_call(
        _mem_kv_kernel,
        grid=(BATCH,),
        in_specs=[blk, pl.BlockSpec((1, D_MODEL), lambda b: (0, 0)),
                  pl.BlockSpec((D_MODEL, 2 * D_MODEL), lambda b: (0, 0))],
        out_specs=[blk, blk],
        out_shape=[jax.ShapeDtypeStruct((BATCH, MEM_LEN, D_MODEL), BF16)] * 2,
        compiler_params=pltpu.CompilerParams(
            dimension_semantics=("arbitrary",), vmem_limit_bytes=48 * 1024 * 1024),
        name="mem_kv",
    )(mem, g, w_ckv)


def _mid_kernel(x_ref, yp_ref, yd_ref, wo_ref, g2_ref, wcq_ref, kx_ref, vx_ref, wco_ref, g3_ref,
                wrh_ref, wrl_ref, br_ref, tri_ref,
                x2_ref, xn_ref, idx_ref, gate_ref, rank_ref, cnt_ref, attn_ref, base_ref):
    @pl.when(pl.program_id(0) == 0)
    def _():
        base_ref[...] = jnp.zeros_like(base_ref)

    xn_groups = []
    for r0 in range(0, TM_MID, MID_ROWS):
        rows = slice(r0, r0 + MID_ROWS)
        x1 = (x_ref[rows, :] + _dot(yp_ref[rows, :], wo_ref[0:POOL_WIDTH, :])
              + _dot(yd_ref[rows, :], wo_ref[POOL_WIDTH:, :]))
        h2 = _rms(x1, g2_ref[...]).astype(BF16)
        qx = (_dot(h2, wcq_ref[...]) * (X_HEAD_DIM ** -0.5)).astype(BF16)
        for hd in range(X_HEADS):
            cols = slice(hd * X_HEAD_DIM, (hd + 1) * X_HEAD_DIM)
            s = _dot_nt(qx[:, cols], kx_ref[0, :, cols])
            p = jnp.exp(s - jnp.max(s, axis=-1, keepdims=True))
            p = p / jnp.sum(p, axis=-1, keepdims=True)
            attn_ref[rows, cols] = _dot(p.astype(BF16), vx_ref[0, :, cols]).astype(BF16)
        x2 = x1 + _dot(attn_ref[rows, :], wco_ref[...])
        x2_ref[rows, :] = x2
        xn_g = _rms(x2, g3_ref[...])
        xn_ref[rows, :] = xn_g
        xn_groups.append(xn_g)
    xn = jnp.concatenate(xn_groups, axis=0)

    xh = xn.astype(BF16)
    xl = (xn - xh.astype(F32)).astype(BF16)
    logits = (_dot_nt(wrh_ref[...], xh) + _dot_nt(wrh_ref[...], xl) + _dot_nt(wrl_ref[...], xh)
              + br_ref[...])
    eidx = lax.broadcasted_iota(I32, (N_EXPERTS, TM_MID), 0).astype(F32)
    vals, hots = [], []
    for k in range(TOP_K):
        mx = jnp.max(logits, axis=0, keepdims=True)
        sel = jnp.min(jnp.where(logits == mx, eidx, float(N_EXPERTS)), axis=0, keepdims=True)
        hot = eidx == sel
        idx_ref[k:k + 1, :] = sel.astype(I32)
        vals.append(mx)
        hots.append(hot)
        logits = jnp.where(hot, -jnp.inf, logits)
    ex = [jnp.exp(v - vals[0]) for v in vals]
    den = ex[0] + ex[1] + ex[2] + ex[3]
    for k in range(TOP_K):
        gate_ref[k:k + 1, :] = ex[k] / den

    chosen = jnp.zeros((N_EXPERTS, TM_MID), F32)
    for k in range(TOP_K):
        chosen = chosen + jnp.where(hots[k], 1.0, 0.0)
    before = _dot(chosen.astype(BF16), tri_ref[...]) + base_ref[:, 0:1]
    for k in range(TOP_K):
        rank_ref[k:k + 1, :] = jnp.sum(jnp.where(hots[k], before, 0.0), axis=0, keepdims=True).astype(I32)
    base_ref[...] = base_ref[...] + jnp.sum(chosen, axis=1, keepdims=True)
    cnt_ref[...] = base_ref[...]


def _mid(x2d, y_pool, y_diff, w_out, g2, w_cq, kx, vx, w_co, g3, wr_hi, wr_lo, b_r, tri):
    n = N_TOK // TM_MID
    per_batch = SEQ // TM_MID
    tok = lambda i: (i, 0)
    fixed = lambda i: (0, 0)
    kv = pl.BlockSpec((1, MEM_LEN, D_MODEL), lambda i: (i // per_batch, 0, 0))
    row4 = pl.BlockSpec((TOP_K, TM_MID), lambda i: (0, i))
    return pl.pallas_call(
        _mid_kernel,
        grid=(n,),
        in_specs=[
            pl.BlockSpec((TM_MID, D_MODEL), tok),
            pl.BlockSpec((TM_MID, POOL_WIDTH), tok),
            pl.BlockSpec((TM_MID, DIFF_WIDTH), tok),
            pl.BlockSpec((D_MODEL, D_MODEL), fixed),
            pl.BlockSpec((1, D_MODEL), fixed),
            pl.BlockSpec((D_MODEL, D_MODEL), fixed),
            kv, kv,
            pl.BlockSpec((D_MODEL, D_MODEL), fixed),
            pl.BlockSpec((1, D_MODEL), fixed),
            pl.BlockSpec((N_EXPERTS, D_MODEL), fixed),
            pl.BlockSpec((N_EXPERTS, D_MODEL), fixed),
            pl.BlockSpec((N_EXPERTS, 1), fixed),
            pl.BlockSpec((TM_MID, TM_MID), fixed),
        ],
        out_specs=[
            pl.BlockSpec((TM_MID, D_MODEL), tok),
            pl.BlockSpec((TM_MID, D_MODEL), tok),
            row4, row4, row4,
            pl.BlockSpec((N_EXPERTS, LANES), fixed),
        ],
        out_shape=[
            jax.ShapeDtypeStruct((N_TOK, D_MODEL), F32),
            jax.ShapeDtypeStruct((N_TOK, D_MODEL), F32),
            jax.ShapeDtypeStruct((TOP_K, N_TOK), I32),
            jax.ShapeDtypeStruct((TOP_K, N_TOK), F32),
            jax.ShapeDtypeStruct((TOP_K, N_TOK), I32),
            jax.ShapeDtypeStruct((N_EXPERTS, LANES), F32),
        ],
        scratch_shapes=[pltpu.VMEM((TM_MID, D_MODEL), BF16), pltpu.VMEM((N_EXPERTS, LANES), F32)],
        compiler_params=pltpu.CompilerParams(
            dimension_semantics=("arbitrary",), vmem_limit_bytes=56 * 1024 * 1024),
        name="mid_block",
    )(x2d, y_pool, y_diff, w_out, g2, w_cq, kx, vx, w_co, g3, wr_hi, wr_lo, b_r, tri)


def _for_rows(n_rows, fn):
    def body(g, c):
        for j in range(ROW_UNROLL):
            fn(g * ROW_UNROLL + j)
        return c

    lax.fori_loop(0, n_rows // ROW_UNROLL, body, 0)


def _to_row_tiles(dst_ref, val, row0=0):
    n = val.shape[0]
    for s in range(ROW_TILE):
        dst_ref[pl.ds(row0 * ROW_TILE + s, n, stride=ROW_TILE), :] = val[:, s * LANES:(s + 1) * LANES]


def _from_row_tiles(src_ref, n, row0=0):
    return jnp.concatenate(
        [src_ref[pl.ds(row0 * ROW_TILE + s, n, stride=ROW_TILE), :] for s in range(ROW_TILE)], axis=1)


def _tile_rows(row):
    return pl.ds(pl.multiple_of(row * ROW_TILE, ROW_TILE), ROW_TILE)


def _dispatch_kernel(dest_ref, xn_ref, xs_ref, stage_ref, sems):
    i = pl.program_id(0)
    last = pl.num_programs(0) - 1

    def copies(step, slot, r):
        return [pltpu.make_async_copy(stage_ref.at[slot, _tile_rows(r)],
                                      xs_ref.at[_tile_rows(dest_ref[(step * TM_DISP + r) * TOP_K + k])],
                                      sems.at[slot]) for k in range(TOP_K)]

    def start(step, slot):
        def fn(r):
            for k, cp in enumerate(copies(step, slot, r)):
                cp.start(priority=k % 2)
        _for_rows(TM_DISP, fn)

    def wait(step, slot):
        def fn(r):
            for cp in copies(step, slot, r):
                cp.wait()
        _for_rows(TM_DISP, fn)

    for slot in range(2):
        @pl.when(i % 2 == slot)
        def _():
            _to_row_tiles(stage_ref.at[slot], xn_ref[...])
            start(i, slot)

            @pl.when(i > 0)
            def _():
                wait(i - 1, 1 - slot)

            @pl.when(i == last)
            def _():
                wait(i, slot)


def _dispatch(dest_flat, xn):
    return pl.pallas_call(
        _dispatch_kernel,
        grid_spec=pltpu.PrefetchScalarGridSpec(
            num_scalar_prefetch=1,
            grid=(N_TOK // TM_DISP,),
            in_specs=[pl.BlockSpec((TM_DISP, D_MODEL), lambda i, d: (i, 0))],
            out_specs=pl.BlockSpec(memory_space=pl.ANY),
            scratch_shapes=[pltpu.VMEM((2, TM_DISP * ROW_TILE, LANES), F32), pltpu.SemaphoreType.DMA((2,))],
        ),
        out_shape=jax.ShapeDtypeStruct((N_ROWS * ROW_TILE, LANES), F32),
        compiler_params=pltpu.CompilerParams(dimension_semantics=("arbitrary",)),
        name="moe_dispatch",
    )(dest_flat, xn)


def _gmm_kernel(blk_ref, exp_ref, lo_ref, hi_ref, first_ref, newe_ref, nexte_ref, xs_ref, wgu_ref, bgu_ref, wd_ref,
                bd_ref, o_ref, wgu_f, wd_f, wgu_s, wd_s, sems):
    w = pl.program_id(0)
    lo = lo_ref[w]
    hi = hi_ref[w]

    def weight_copies(e):
        return (pltpu.make_async_copy(wgu_ref.at[e], wgu_f, sems.at[0]),
                pltpu.make_async_copy(wd_ref.at[e], wd_f, sems.at[1]))

    @pl.when(w == 0)
    def _():
        for cp in weight_copies(exp_ref[0]):
            cp.start()

    @pl.when(newe_ref[w] == 1)
    def _():
        for cp in weight_copies(exp_ref[w]):
            cp.wait()
        wgu_s[...] = wgu_f[...].astype(BF16)
        wd_s[...] = wd_f[...].astype(BF16)

        @pl.when(nexte_ref[w] < N_EXPERTS)
        def _():
            for cp in weight_copies(nexte_ref[w]):
                cp.start()

    def expert_ffn(merge):
        for r0 in range(0, TM_GMM, GMM_ROWS):
            xb = _from_row_tiles(xs_ref, GMM_ROWS, r0).astype(BF16)
            gate = _dot(xb, wgu_s[:, 0:D_EXPERT]) + bgu_ref[0, :, 0:D_EXPERT]
            up = _dot(xb, wgu_s[:, D_EXPERT:]) + bgu_ref[0, :, D_EXPERT:]
            gate = jnp.minimum(gate, SWIGLU_LIMIT)
            up = jnp.clip(up, -SWIGLU_LIMIT, SWIGLU_LIMIT)
            hid = (up + 1.0) * gate * jax.nn.sigmoid(SWIGLU_ALPHA * gate)
            y = _dot(hid.astype(BF16), wd_s[...]) + bd_ref[0]
            row = r0 + lax.broadcasted_iota(I32, (GMM_ROWS, 1), 0)
            mine = (row >= lo) & (row < hi)
            other = _from_row_tiles(o_ref, GMM_ROWS, r0) if merge else 0.0
            _to_row_tiles(o_ref, jnp.where(mine, y, other), r0)

    @pl.when((hi > lo) & (first_ref[w] == 1))
    def _():
        expert_ffn(merge=False)

    @pl.when((hi > lo) & (first_ref[w] == 0))
    def _():
        expert_ffn(merge=True)


def _gmm(item_blk, item_exp, item_lo, item_hi, item_first, item_newe, item_nexte, xs, w_gu, b_gu, w_down, b_down):
    rows = lambda w, blk, e, lo, hi, f, ne, nx: (blk[w], 0)
    per_e = lambda w, blk, e, lo, hi, f, ne, nx: (e[w], 0, 0)
    return pl.pallas_call(
        _gmm_kernel,
        grid_spec=pltpu.PrefetchScalarGridSpec(
            num_scalar_prefetch=7,
            grid=(N_ITEMS,),
            in_specs=[
                pl.BlockSpec((TM_GMM * ROW_TILE, LANES), rows),
                pl.BlockSpec(memory_space=pl.ANY),
                pl.BlockSpec((1, 1, 2 * D_EXPERT), per_e),
                pl.BlockSpec(memory_space=pl.ANY),
                pl.BlockSpec((1, 1, D_MODEL), per_e),
            ],
            out_specs=pl.BlockSpec((TM_GMM * ROW_TILE, LANES), rows),
            scratch_shapes=[pltpu.VMEM((D_MODEL, 2 * D_EXPERT), F32), pltpu.VMEM((D_EXPERT, D_MODEL), F32),
                            pltpu.VMEM((D_MODEL, 2 * D_EXPERT), BF16), pltpu.VMEM((D_EXPERT, D_MODEL), BF16),
                            pltpu.SemaphoreType.DMA((2,))],
        ),
        out_shape=jax.ShapeDtypeStruct((N_ROWS * ROW_TILE, LANES), F32),
        compiler_params=pltpu.CompilerParams(
            dimension_semantics=("arbitrary",), vmem_limit_bytes=56 * 1024 * 1024),
        name="moe_experts",
    )(item_blk, item_exp, item_lo, item_hi, item_first, item_newe, item_nexte, xs, w_gu, b_gu, w_down, b_down)


def _combine_kernel(dest_ref, ys_ref, x2_ref, gate_ref, g_ref, o_ref, buf0_ref, buf1_ref, sems):
    i = pl.program_id(0)
    last = pl.num_programs(0) - 1
    bufs = (buf0_ref, buf1_ref)

    def copies(step, slot, r):
        return [pltpu.make_async_copy(ys_ref.at[_tile_rows(dest_ref[(step * TM_COMB + r) * TOP_K + k])],
                                      bufs[slot].at[k, _tile_rows(r)],
                                      sems.at[slot]) for k in range(TOP_K)]

    def start(step, slot):
        def fn(r):
            for k, cp in enumerate(copies(step, slot, r)):
                cp.start(priority=k % 2)
        _for_rows(TM_COMB, fn)

    def wait(step, slot):
        def fn(r):
            for cp in copies(step, slot, r):
                cp.wait()
        _for_rows(TM_COMB, fn)

    @pl.when(i == 0)
    def _():
        start(0, 0)

    nxt = jnp.minimum(i + 1, last)
    for slot in range(2):
        @pl.when(i % 2 == slot)
        def _():
            wait(i, slot)
            for r0 in range(0, TM_COMB, COMB_ROWS):
                for r in range(r0, r0 + COMB_ROWS):
                    for k, cp in enumerate(copies(nxt, 1 - slot, r)):
                        cp.start(priority=k % 2)
                rows = slice(r0, r0 + COMB_ROWS)
                gates = gate_ref[rows, :]
                y = x2_ref[rows, :]
                for k in range(TOP_K):
                    y = y + _from_row_tiles(bufs[slot].at[k], COMB_ROWS, r0) * gates[:, k:k + 1]
                o_ref[rows, :] = _rms(y, g_ref[...])

            @pl.when(i == last)
            def _():
                wait(nxt, 1 - slot)


def _combine(dest_flat, ys, x2, gates_t, g):
    tok = lambda i, d: (i, 0)
    return pl.pallas_call(
        _combine_kernel,
        grid_spec=pltpu.PrefetchScalarGridSpec(
            num_scalar_prefetch=1,
            grid=(N_TOK // TM_COMB,),
            in_specs=[
                pl.BlockSpec(memory_space=pl.ANY),
                pl.BlockSpec((TM_COMB, D_MODEL), tok),
                pl.BlockSpec((TM_COMB, TOP_K), tok),
                pl.BlockSpec((1, D_MODEL), lambda i, d: (0, 0)),
            ],
            out_specs=pl.BlockSpec((TM_COMB, D_MODEL), tok),
            scratch_shapes=[pltpu.VMEM((TOP_K, TM_COMB * ROW_TILE, LANES), F32),
                            pltpu.VMEM((TOP_K, TM_COMB * ROW_TILE, LANES), F32),
                            pltpu.SemaphoreType.DMA((2,))],
        ),
        out_shape=jax.ShapeDtypeStruct((N_TOK, D_MODEL), F32),
        compiler_params=pltpu.CompilerParams(
            dimension_semantics=("arbitrary",), vmem_limit_bytes=48 * 1024 * 1024),
        name="moe_combine",
    )(dest_flat, ys, x2, gates_t, g)


def _rotary_lane_table():
    half = ROT_DIM // 2
    inv_freq = ROPE_THETA ** (-jnp.arange(0, ROT_DIM, 2, dtype=F32) / ROT_DIM)
    rest = jnp.zeros((DIFF_HEAD_DIM - ROT_DIM,), F32)
    zh = jnp.zeros((half,), F32)
    oh = jnp.ones((half,), F32)
    reps = LANES // DIFF_HEAD_DIM
    rows = [jnp.tile(jnp.concatenate(r), reps) for r in
            ([inv_freq, inv_freq, rest], [-oh, zh, rest], [zh, oh, rest])]
    return jnp.concatenate([jnp.stack(rows), jnp.zeros((8 - len(rows), LANES), F32)])


def _work_items(counts):
    ends = jnp.cumsum(counts)
    starts = ends - counts
    first_blk = starts // TM_GMM
    last_blk = jnp.maximum(ends - 1, 0) // TM_GMM
    n_items = jnp.where(counts > 0, last_blk - first_blk + 1, 0)
    item_end = jnp.cumsum(n_items)
    item_start = item_end - n_items
    total = item_end[-1]
    w = jnp.arange(N_ITEMS, dtype=I32)
    valid = w < total
    wc = jnp.minimum(w, total - 1)
    e = jnp.minimum(jnp.sum((item_end[None, :] <= wc[:, None]).astype(I32), axis=1), N_EXPERTS - 1)
    is_e = e[:, None] == jnp.arange(N_EXPERTS, dtype=I32)[None, :]
    pick = lambda per_expert: jnp.sum(jnp.where(is_e, per_expert[None, :], 0), axis=1)
    blk = jnp.where(valid, pick(first_blk) + (w - pick(item_start)), N_BLOCKS - 1).astype(I32)
    lo = jnp.where(valid, jnp.maximum(pick(starts) - blk * TM_GMM, 0), 0).astype(I32)
    hi = jnp.where(valid, jnp.minimum(pick(ends) - blk * TM_GMM, TM_GMM), 0).astype(I32)
    first = (blk != jnp.concatenate([jnp.full((1,), -1, I32), blk[:-1]])).astype(I32)
    new_e = (valid & (e != jnp.concatenate([jnp.full((1,), -1, I32), e[:-1]]))).astype(I32)
    ids = jnp.arange(N_EXPERTS, dtype=I32)
    later = (ids[None, :] > ids[:, None]) & (counts[None, :] > 0)
    next_of = jnp.min(jnp.where(later, ids[None, :], N_EXPERTS), axis=1).astype(I32)
    next_e = jnp.where(valid, pick(next_of), N_EXPERTS).astype(I32)
    return blk, e, lo, hi, first, new_e, next_e, starts


def kernel(x, positions, mem, attn_norm_g, w_in, w_pool, pool_scale, lambda_q1, lambda_k1, lambda_q2, lambda_k2, subln_g, w_out, xattn_norm_g, mem_norm_g, w_cq, w_ckv, w_co, ffn_norm_g, w_router, b_router, w_gu, b_gu, w_down, b_down, final_norm_g):
    l = 0
    x2d = x.reshape(N_TOK, D_MODEL)
    u_pool, q, k, v = _in_proj(x2d, attn_norm_g[l].reshape(1, D_MODEL), w_in[l].astype(BF16),
                               positions.reshape(N_TOK, 1), _rotary_lane_table())
    y_pool = _pool(u_pool.reshape(BATCH, SEQ, POOL_WIDTH), w_pool[l].astype(BF16),
                   pool_scale[l].reshape(1, POOL_WIDTH))
    lam_vecs = jnp.stack([lambda_q1[l], lambda_k1[l], lambda_q2[l], lambda_k2[l]]).astype(F32)
    y_diff = _diff_attn(lam_vecs, q.reshape(BATCH, SEQ, QK_WIDTH), k.reshape(BATCH, SEQ, QK_WIDTH),
                        v.reshape(BATCH, SEQ, DIFF_WIDTH), subln_g[l].reshape(1, DIFF_V_DIM))
    kx, vx = _mem_kv(mem, mem_norm_g[l].reshape(1, D_MODEL), w_ckv[l].astype(BF16))

    wr_t = w_router[l].T
    wr_hi = wr_t.astype(BF16)
    wr_lo = (wr_t - wr_hi.astype(F32)).astype(BF16)
    tri = (jnp.arange(TM_MID)[:, None] < jnp.arange(TM_MID)[None, :]).astype(BF16)
    x2, xn, top_idx, gates, rank, cnt = _mid(
        x2d, y_pool.reshape(N_TOK, POOL_WIDTH), y_diff.reshape(N_TOK, DIFF_WIDTH), w_out[l].astype(BF16),
        xattn_norm_g[l].reshape(1, D_MODEL), w_cq[l].astype(BF16), kx, vx, w_co[l].astype(BF16),
        ffn_norm_g[l].reshape(1, D_MODEL), wr_hi, wr_lo, b_router[l].reshape(N_EXPERTS, 1), tri)

    counts = cnt[:, 0].astype(I32)
    item_blk, item_exp, item_lo, item_hi, item_first, item_newe, item_nexte, starts = _work_items(counts)
    hot = top_idx[:, :, None] == jnp.arange(N_EXPERTS, dtype=I32)
    dest = jnp.sum(jnp.where(hot, starts, 0), axis=-1) + rank
    dest_flat = dest.T.reshape(N_ROWS)

    xs = _dispatch(dest_flat, xn)
    ys = _gmm(item_blk, item_exp, item_lo, item_hi, item_first, item_newe, item_nexte, xs,
              w_gu[l], b_gu[l].reshape(N_EXPERTS, 1, 2 * D_EXPERT),
              w_down[l], b_down[l].reshape(N_EXPERTS, 1, D_MODEL))
    out = _combine(dest_flat, ys, x2, gates.T, final_norm_g.reshape(1, D_MODEL))
    return out.reshape(BATCH, SEQ, D_MODEL)
```

```python
import functools
import math

import jax
import jax.numpy as jnp
from jax import lax
from jax.experimental import pallas as pl
from jax.experimental.pallas import tpu as pltpu

F32 = jnp.float32
BF16 = jnp.bfloat16
I32 = jnp.int32

D_MODEL = 1024
BATCH = 8
SEQ = 2048
N_TOK = BATCH * SEQ
CHUNK = 64
NORM_EPS = 1e-5
POOL_WIDTH = 512
POOL_WINDOWS = (2, 4, 8, 16)
POOL_GROUP_DIM = 128
MAX_WINDOW = max(POOL_WINDOWS)
DIFF_HEADS = 4
DIFF_HEAD_DIM = 64
DIFF_V_DIM = 128
DIFF_WIDTH = 512
QK_WIDTH = 512
IN_WIDTH = 2048
ROT_DIM = 16
ROPE_THETA = 500000.0
MEM_LEN = 256
X_HEADS = 4
X_HEAD_DIM = 256
N_EXPERTS = 32
TOP_K = 4
D_EXPERT = 1024
SWIGLU_ALPHA = 1.702
SWIGLU_LIMIT = 7.0
LAM_INIT = 0.8 - 0.6 * math.exp(-0.3 * 0)
N_ROWS = N_TOK * TOP_K

LANES = 128

TM_IN = 512
IN_ROWS = 512
TQ = 256
POOL_ROWS = 512
TM_MID = 512
MID_ROWS = 512
TM_DISP = 256
TM_GMM = 512
GMM_ROWS = 256
TM_COMB = 256
COMB_ROWS = 32
ROW_UNROLL = 8
ROW_TILE = D_MODEL // LANES
N_PAD_ROWS = N_ROWS + N_EXPERTS * TM_GMM
N_ITEMS = N_PAD_ROWS // TM_GMM


def _rms(xf, g):
    ms = jnp.mean(xf * xf, axis=-1, keepdims=True)
    return xf * lax.rsqrt(ms + NORM_EPS) * g


def _dot(a, b):
    return jnp.dot(a, b, preferred_element_type=F32)


def _dot_nt(a, b):
    return lax.dot_general(a, b, (((1,), (1,)), ((), ())), preferred_element_type=F32)


def _in_proj_kernel(x_ref, g_ref, w_ref, pos_ref, rot_ref, up_ref, q_ref, k_ref, v_ref):
    for r0 in range(0, TM_IN, IN_ROWS):
        rows = slice(r0, r0 + IN_ROWS)
        h = _rms(x_ref[rows, :], g_ref[...]).astype(BF16)
        ang = pos_ref[rows, :].astype(F32) * rot_ref[0:1, :]
        cosf = jnp.cos(ang)
        sinf = jnp.sin(ang)
        sa = sinf * rot_ref[1:2, :]
        sb = sinf * rot_ref[2:3, :]
        up_ref[rows, :] = _dot(h, w_ref[:, 0:POOL_WIDTH])
        for off, o_ref, scale in ((POOL_WIDTH, q_ref, DIFF_HEAD_DIM ** -0.5),
                                  (POOL_WIDTH + QK_WIDTH, k_ref, 1.0)):
            u = _dot(h, w_ref[:, off:off + QK_WIDTH])
            for hd in range(DIFF_HEADS):
                uh = u[:, hd * LANES:(hd + 1) * LANES]
                r = uh * cosf + pltpu.roll(uh, LANES - 8, 1) * sa + pltpu.roll(uh, 8, 1) * sb
                o_ref[rows, hd * LANES:(hd + 1) * LANES] = (r * scale).astype(BF16)
        v_ref[rows, :] = _dot(h, w_ref[:, POOL_WIDTH + 2 * QK_WIDTH:IN_WIDTH]).astype(BF16)


def _in_proj(x2d, g, w_in, pos, rot_tab):
    n = N_TOK // TM_IN
    tok = lambda i: (i, 0)
    fixed = lambda i: (0, 0)
    return pl.pallas_call(
        _in_proj_kernel,
        grid=(n,),
        in_specs=[
            pl.BlockSpec((TM_IN, D_MODEL), tok),
            pl.BlockSpec((1, D_MODEL), fixed),
            pl.BlockSpec((D_MODEL, IN_WIDTH), fixed),
            pl.BlockSpec((TM_IN, 1), tok),
            pl.BlockSpec((8, LANES), fixed),
        ],
        out_specs=[
            pl.BlockSpec((TM_IN, POOL_WIDTH), tok),
            pl.BlockSpec((TM_IN, QK_WIDTH), tok),
            pl.BlockSpec((TM_IN, QK_WIDTH), tok),
            pl.BlockSpec((TM_IN, DIFF_WIDTH), tok),
        ],
        out_shape=[
            jax.ShapeDtypeStruct((N_TOK, POOL_WIDTH), F32),
            jax.ShapeDtypeStruct((N_TOK, QK_WIDTH), BF16),
            jax.ShapeDtypeStruct((N_TOK, QK_WIDTH), BF16),
            jax.ShapeDtypeStruct((N_TOK, DIFF_WIDTH), BF16),
        ],
        compiler_params=pltpu.CompilerParams(
            dimension_semantics=("arbitrary",), vmem_limit_bytes=48 * 1024 * 1024),
        name="in_proj",
    )(x2d, g, w_in, pos, rot_tab)


def _pool_kernel(u_ref, w_ref, sc_ref, o_ref, pad_ref):
    pad_ref[0:MAX_WINDOW, :] = jnp.zeros((MAX_WINDOW, POOL_WIDTH), F32)
    pad_ref[MAX_WINDOW:, :] = u_ref[0]
    for g, win in enumerate(POOL_WINDOWS):
        lanes = slice(g * POOL_GROUP_DIM, (g + 1) * POOL_GROUP_DIM)
        for c in range(SEQ // POOL_ROWS):
            r0 = c * POOL_ROWS
            u = pad_ref[MAX_WINDOW + r0:MAX_WINDOW + r0 + POOL_ROWS, lanes]
            acc = u
            for j in range(1, win):
                acc = acc + pad_ref[MAX_WINDOW + r0 - j:MAX_WINDOW + r0 - j + POOL_ROWS, lanes]
            t = r0 + lax.broadcasted_iota(I32, (POOL_ROWS, 1), 0)
            cnt = jnp.minimum(t + 1, win).astype(F32)
            mixed = (acc / cnt - u).astype(BF16)
            y = _dot(mixed, w_ref[g]) * sc_ref[:, lanes]
            o_ref[0, r0:r0 + POOL_ROWS, lanes] = y.astype(BF16)


def _pool(u_pool, w_pool, pool_scale):
    return pl.pallas_call(
        _pool_kernel,
        grid=(BATCH,),
        in_specs=[
            pl.BlockSpec((1, SEQ, POOL_WIDTH), lambda b: (b, 0, 0)),
            pl.BlockSpec((len(POOL_WINDOWS), POOL_GROUP_DIM, POOL_GROUP_DIM), lambda b: (0, 0, 0)),
            pl.BlockSpec((1, POOL_WIDTH), lambda b: (0, 0)),
        ],
        out_specs=pl.BlockSpec((1, SEQ, POOL_WIDTH), lambda b: (b, 0, 0)),
        out_shape=jax.ShapeDtypeStruct((BATCH, SEQ, POOL_WIDTH), BF16),
        scratch_shapes=[pltpu.VMEM((SEQ + MAX_WINDOW, POOL_WIDTH), F32)],
        compiler_params=pltpu.CompilerParams(
            dimension_semantics=("arbitrary",), vmem_limit_bytes=48 * 1024 * 1024),
        name="pool_mixer",
    )(u_pool, w_pool, pool_scale)


def _diff_attn_kernel(lam_ref, q_ref, k_ref, v_ref, g_ref, o_ref):
    lv = lam_ref[...]
    e1 = jnp.exp(jnp.sum(lv[0:1] * lv[1:2], axis=-1, keepdims=True))
    e2 = jnp.exp(jnp.sum(lv[2:3] * lv[3:4], axis=-1, keepdims=True))
    lam = e1 - e2 + LAM_INIT
    lane = lax.broadcasted_iota(I32, (TQ, LANES), 1)
    qc = lax.broadcasted_iota(I32, (TQ, TQ), 0) // CHUNK
    kc = lax.broadcasted_iota(I32, (TQ, TQ), 1) // CHUNK
    diag_mask = kc <= qc
    gain = g_ref[...] * (1.0 - LAM_INIT)
    zero = jnp.zeros((), BF16)
    for qi in range(SEQ // TQ):
        q0 = qi * TQ
        qt = q_ref[0, q0:q0 + TQ, :]
        qmaps = (jnp.where(lane < DIFF_HEAD_DIM, qt, zero), jnp.where(lane >= DIFF_HEAD_DIM, qt, zero))
        kd = k_ref[0, q0:q0 + TQ, :]
        s_diag = [jnp.where(diag_mask, _dot_nt(qm, kd), -jnp.inf) for qm in qmaps]
        if qi > 0:
            ka = k_ref[0, 0:q0, :]
            s_past = [_dot_nt(qm, ka) for qm in qmaps]
        p_diag, p_past, inv = [], [], []
        for m in range(2):
            mx = jnp.max(s_diag[m], axis=-1, keepdims=True)
            if qi > 0:
                mx = jnp.maximum(mx, jnp.max(s_past[m], axis=-1, keepdims=True))
            pd = jnp.exp(s_diag[m] - mx)
            den = jnp.sum(pd, axis=-1, keepdims=True)
            p_diag.append(pd)
            if qi > 0:
                pp = jnp.exp(s_past[m] - mx)
                den = den + jnp.sum(pp, axis=-1, keepdims=True)
                p_past.append(pp)
            inv.append(1.0 / den)
        c1 = inv[0]
        c2 = lam * inv[1]
        a_diag = (p_diag[0] * c1 - p_diag[1] * c2).astype(BF16)
        o = _dot(a_diag, v_ref[0, q0:q0 + TQ, :])
        if qi > 0:
            a_past = (p_past[0] * c1 - p_past[1] * c2).astype(BF16)
            o = o + _dot(a_past, v_ref[0, 0:q0, :])
        o_ref[0, q0:q0 + TQ, :] = _rms(o, gain).astype(BF16)


def _diff_attn(lam_vecs, q, k, v, subln_g):
    blk = pl.BlockSpec((1, SEQ, LANES), lambda b, h: (b, 0, h))
    return pl.pallas_call(
        _diff_attn_kernel,
        grid=(BATCH, DIFF_HEADS),
        in_specs=[
            pl.BlockSpec((4, DIFF_HEAD_DIM), lambda b, h: (0, 0)),
            blk, blk, blk,
            pl.BlockSpec((1, DIFF_V_DIM), lambda b, h: (0, 0)),
        ],
        out_specs=blk,
        out_shape=jax.ShapeDtypeStruct((BATCH, SEQ, DIFF_WIDTH), BF16),
        compiler_params=pltpu.CompilerParams(
            dimension_semantics=("arbitrary", "arbitrary"), vmem_limit_bytes=48 * 1024 * 1024),
        name="diff_attn",
    )(lam_vecs, q, k, v, subln_g)


def _mem_kv_kernel(m_ref, g_ref, w_ref, k_ref, v_ref):
    h = _rms(m_ref[0], g_ref[...]).astype(BF16)
    k_ref[0] = _dot(h, w_ref[:, 0:D_MODEL]).astype(BF16)
    v_ref[0] = _dot(h, w_ref[:, D_MODEL:2 * D_MODEL]).astype(BF16)


def _mem_kv(mem, g, w_ckv):
    blk = pl.BlockSpec((1, MEM_LEN, D_MODEL), lambda b: (b, 0, 0))
    return pl.pallas_call(
        _mem_kv_kernel,
        grid=(BATCH,),
        in_specs=[blk, pl.BlockSpec((1, D_MODEL), lambda b: (0, 0)),
                  pl.BlockSpec((D_MODEL, 2 * D_MODEL), lambda b: (0, 0))],
        out_specs=[blk, blk],
        out_shape=[jax.ShapeDtypeStruct((BATCH, MEM_LEN, D_MODEL), BF16)] * 2,
        compiler_params=pltpu.CompilerParams(
            dimension_semantics=("arbitrary",), vmem_limit_bytes=48 * 1024 * 1024),
        name="mem_kv",
    )(mem, g, w_ckv)


def _mid_kernel(x_ref, yp_ref, yd_ref, wo_ref, g2_ref, wcq_ref, kx_ref, vx_ref, wco_ref, g3_ref,
                wrh_ref, wrl_ref, br_ref, tri_ref,
                x2_ref, xn_ref, idx_ref, gate_ref, rank_ref, cnt_ref, attn_ref, base_ref):
    @pl.when(pl.program_id(0) == 0)
    def _():
        base_ref[...] = jnp.zeros_like(base_ref)

    xn_groups = []
    for r0 in range(0, TM_MID, MID_ROWS):
        rows = slice(r0, r0 + MID_ROWS)
        x1 = (x_ref[rows, :] + _dot(yp_ref[rows, :], wo_ref[0:POOL_WIDTH, :])
              + _dot(yd_ref[rows, :], wo_ref[POOL_WIDTH:, :]))
        h2 = _rms(x1, g2_ref[...]).astype(BF16)
        qx = (_dot(h2, wcq_ref[...]) * (X_HEAD_DIM ** -0.5)).astype(BF16)
        for hd in range(X_HEADS):
            cols = slice(hd * X_HEAD_DIM, (hd + 1) * X_HEAD_DIM)
            s = _dot_nt(qx[:, cols], kx_ref[0, :, cols])
            p = jnp.exp(s - jnp.max(s, axis=-1, keepdims=True))
            p = p / jnp.sum(p, axis=-1, keepdims=True)
            attn_ref[rows, cols] = _dot(p.astype(BF16), vx_ref[0, :, cols]).astype(BF16)
        x2 = x1 + _dot(attn_ref[rows, :], wco_ref[...])
        x2_ref[rows, :] = x2
        xn_g = _rms(x2, g3_ref[...])
        xn_ref[rows, :] = xn_g
        xn_groups.append(xn_g)
    xn = jnp.concatenate(xn_groups, axis=0)

    xh = xn.astype(BF16)
    xl = (xn - xh.astype(F32)).astype(BF16)
    logits = (_dot_nt(wrh_ref[...], xh) + _dot_nt(wrh_ref[...], xl) + _dot_nt(wrl_ref[...], xh)
              + br_ref[...])
    eidx = lax.broadcasted_iota(I32, (N_EXPERTS, TM_MID), 0).astype(F32)
    vals, hots = [], []
    for k in range(TOP_K):
        mx = jnp.max(logits, axis=0, keepdims=True)
        sel = jnp.min(jnp.where(logits == mx, eidx, float(N_EXPERTS)), axis=0, keepdims=True)
        hot = eidx == sel
        idx_ref[k:k + 1, :] = sel.astype(I32)
        vals.append(mx)
        hots.append(hot)
        logits = jnp.where(hot, -jnp.inf, logits)
    ex = [jnp.exp(v - vals[0]) for v in vals]
    den = ex[0] + ex[1] + ex[2] + ex[3]
    for k in range(TOP_K):
        gate_ref[k:k + 1, :] = ex[k] / den

    chosen = jnp.zeros((N_EXPERTS, TM_MID), F32)
    for k in range(TOP_K):
        chosen = chosen + jnp.where(hots[k], 1.0, 0.0)
    before = _dot(chosen.astype(BF16), tri_ref[...]) + base_ref[:, 0:1]
    for k in range(TOP_K):
        rank_ref[k:k + 1, :] = jnp.sum(jnp.where(hots[k], before, 0.0), axis=0, keepdims=True).astype(I32)
    base_ref[...] = base_ref[...] + jnp.sum(chosen, axis=1, keepdims=True)
    cnt_ref[...] = base_ref[...]


def _mid(x2d, y_pool, y_diff, w_out, g2, w_cq, kx, vx, w_co, g3, wr_hi, wr_lo, b_r, tri):
    n = N_TOK // TM_MID
    per_batch = SEQ // TM_MID
    tok = lambda i: (i, 0)
    fixed = lambda i: (0, 0)
    kv = pl.BlockSpec((1, MEM_LEN, D_MODEL), lambda i: (i // per_batch, 0, 0))
    row4 = pl.BlockSpec((TOP_K, TM_MID), lambda i: (0, i))
    return pl.pallas_call(
        _mid_kernel,
        grid=(n,),
        in_specs=[
            pl.BlockSpec((TM_MID, D_MODEL), tok),
            pl.BlockSpec((TM_MID, POOL_WIDTH), tok),
            pl.BlockSpec((TM_MID, DIFF_WIDTH), tok),
            pl.BlockSpec((D_MODEL, D_MODEL), fixed),
            pl.BlockSpec((1, D_MODEL), fixed),
            pl.BlockSpec((D_MODEL, D_MODEL), fixed),
            kv, kv,
            pl.BlockSpec((D_MODEL, D_MODEL), fixed),
            pl.BlockSpec((1, D_MODEL), fixed),
            pl.BlockSpec((N_EXPERTS, D_MODEL), fixed),
            pl.BlockSpec((N_EXPERTS, D_MODEL), fixed),
            pl.BlockSpec((N_EXPERTS, 1), fixed),
            pl.BlockSpec((TM_MID, TM_MID), fixed),
        ],
        out_specs=[
            pl.BlockSpec((TM_MID, D_MODEL), tok),
            pl.BlockSpec((TM_MID, D_MODEL), tok),
            row4, row4, row4,
            pl.BlockSpec((N_EXPERTS, LANES), fixed),
        ],
        out_shape=[
            jax.ShapeDtypeStruct((N_TOK, D_MODEL), F32),
            jax.ShapeDtypeStruct((N_TOK, D_MODEL), F32),
            jax.ShapeDtypeStruct((TOP_K, N_TOK), I32),
            jax.ShapeDtypeStruct((TOP_K, N_TOK), F32),
            jax.ShapeDtypeStruct((TOP_K, N_TOK), I32),
            jax.ShapeDtypeStruct((N_EXPERTS, LANES), F32),
        ],
        scratch_shapes=[pltpu.VMEM((TM_MID, D_MODEL), BF16), pltpu.VMEM((N_EXPERTS, LANES), F32)],
        compiler_params=pltpu.CompilerParams(
            dimension_semantics=("arbitrary",), vmem_limit_bytes=56 * 1024 * 1024),
        name="mid_block",
    )(x2d, y_pool, y_diff, w_out, g2, w_cq, kx, vx, w_co, g3, wr_hi, wr_lo, b_r, tri)


def _for_rows(n_rows, fn):
    def body(g, c):
        for j in range(ROW_UNROLL):
            fn(g * ROW_UNROLL + j)
        return c

    lax.fori_loop(0, n_rows // ROW_UNROLL, body, 0)


def _to_row_tiles(dst_ref, val, row0=0):
    n = val.shape[0]
    for s in range(ROW_TILE):
        dst_ref[pl.ds(row0 * ROW_TILE + s, n, stride=ROW_TILE), :] = val[:, s * LANES:(s + 1) * LANES]


def _from_row_tiles(src_ref, n, row0=0):
    return jnp.concatenate(
        [src_ref[pl.ds(row0 * ROW_TILE + s, n, stride=ROW_TILE), :] for s in range(ROW_TILE)], axis=1)


def _tile_rows(row):
    return pl.ds(pl.multiple_of(row * ROW_TILE, ROW_TILE), ROW_TILE)


def _dispatch_kernel(dest_ref, pad_start_ref, pad_len_ref, xn_ref, xs_ref, stage_ref, zero_ref, sems, pad_sem):
    i = pl.program_id(0)
    last = pl.num_programs(0) - 1

    def pad_copies(visit):
        def body(e, c):
            row = pad_start_ref[e]
            n = pad_len_ref[e]
            for bit in reversed(range(TM_GMM.bit_length() - 1)):
                size = 1 << bit
                take = (n & size) != 0
                offset = n & ~(2 * size - 1)

                @pl.when(take)
                def _():
                    visit(pltpu.make_async_copy(
                        zero_ref.at[pl.ds(0, size * ROW_TILE)],
                        xs_ref.at[pl.ds(pl.multiple_of((row + offset) * ROW_TILE, ROW_TILE), size * ROW_TILE)],
                        pad_sem))
            return c
        lax.fori_loop(0, N_EXPERTS, body, 0)

        def unused(b, c):
            @pl.when(b >= pad_len_ref[N_EXPERTS])
            def _():
                for half in range(2):
                    rows = zero_ref.shape[0]
                    visit(pltpu.make_async_copy(
                        zero_ref,
                        xs_ref.at[pl.ds(pl.multiple_of(b * TM_GMM * ROW_TILE + half * rows, rows), rows)],
                        pad_sem))
            return c
        lax.fori_loop(0, N_ITEMS, unused, 0)

    @pl.when(i == 0)
    def _():
        zero_ref[...] = jnp.zeros_like(zero_ref)
        pad_copies(lambda cp: cp.start())

    def copies(step, slot, r):
        return [pltpu.make_async_copy(stage_ref.at[slot, _tile_rows(r)],
                                      xs_ref.at[_tile_rows(dest_ref[(step * TM_DISP + r) * TOP_K + k])],
                                      sems.at[slot]) for k in range(TOP_K)]

    def start(step, slot):
        def fn(r):
            for k, cp in enumerate(copies(step, slot, r)):
                cp.start(priority=k % 2)
        _for_rows(TM_DISP, fn)

    def wait(step, slot):
        def fn(r):
            for cp in copies(step, slot, r):
                cp.wait()
        _for_rows(TM_DISP, fn)

    for slot in range(2):
        @pl.when(i % 2 == slot)
        def _():
            _to_row_tiles(stage_ref.at[slot], xn_ref[...])
            start(i, slot)

            @pl.when(i > 0)
            def _():
                wait(i - 1, 1 - slot)

            @pl.when(i == last)
            def _():
                wait(i, slot)

    @pl.when(i == last)
    def _():
        pad_copies(lambda cp: cp.wait())


def _dispatch(dest_flat, pad_start, pad_len, xn):
    return pl.pallas_call(
        _dispatch_kernel,
        grid_spec=pltpu.PrefetchScalarGridSpec(
            num_scalar_prefetch=3,
            grid=(N_TOK // TM_DISP,),
            in_specs=[pl.BlockSpec((TM_DISP, D_MODEL), lambda i, d, ps, pn: (i, 0))],
            out_specs=pl.BlockSpec(memory_space=pl.ANY),
            scratch_shapes=[pltpu.VMEM((2, TM_DISP * ROW_TILE, LANES), F32),
                            pltpu.VMEM((TM_GMM // 2 * ROW_TILE, LANES), F32),
                            pltpu.SemaphoreType.DMA((2,)), pltpu.SemaphoreType.DMA(())],
        ),
        out_shape=jax.ShapeDtypeStruct((N_PAD_ROWS * ROW_TILE, LANES), F32),
        compiler_params=pltpu.CompilerParams(dimension_semantics=("arbitrary",)),
        name="moe_dispatch",
    )(dest_flat, pad_start, pad_len, xn)


def _gmm_kernel(blk_ref, exp_ref, valid_ref, newe_ref, nexte_ref, xs_ref, wgu_ref, bgu_ref, wd_ref,
                bd_ref, o_ref, wgu_f, wd_f, wgu_s, wd_s, sems):
    w = pl.program_id(0)

    def weight_copies(e):
        return (pltpu.make_async_copy(wgu_ref.at[e], wgu_f, sems.at[0]),
                pltpu.make_async_copy(wd_ref.at[e], wd_f, sems.at[1]))

    @pl.when(w == 0)
    def _():
        for cp in weight_copies(exp_ref[0]):
            cp.start()

    @pl.when(newe_ref[w] == 1)
    def _():
        for cp in weight_copies(exp_ref[w]):
            cp.wait()
        wgu_s[...] = wgu_f[...].astype(BF16)
        wd_s[...] = wd_f[...].astype(BF16)

        @pl.when(nexte_ref[w] < N_EXPERTS)
        def _():
            for cp in weight_copies(nexte_ref[w]):
                cp.start()

    @pl.when(valid_ref[w] == 1)
    def _():
        for r0 in range(0, TM_GMM, GMM_ROWS):
            xb = _from_row_tiles(xs_ref, GMM_ROWS, r0).astype(BF16)
            gate = _dot(xb, wgu_s[:, 0:D_EXPERT]) + bgu_ref[0, :, 0:D_EXPERT]
            up = _dot(xb, wgu_s[:, D_EXPERT:]) + bgu_ref[0, :, D_EXPERT:]
            gate = jnp.minimum(gate, SWIGLU_LIMIT)
            up = jnp.clip(up, -SWIGLU_LIMIT, SWIGLU_LIMIT)
            hid = (up + 1.0) * gate * jax.nn.sigmoid(SWIGLU_ALPHA * gate)
            y = _dot(hid.astype(BF16), wd_s[...]) + bd_ref[0]
            _to_row_tiles(o_ref, y, r0)

    @pl.when(valid_ref[w] == 0)
    def _():
        o_ref[...] = jnp.zeros_like(o_ref)


def _gmm(item_blk, item_exp, item_valid, item_newe, item_nexte, xs, w_gu, b_gu, w_down, b_down):
    rows = lambda w, blk, e, v, ne, nx: (blk[w], 0)
    per_e = lambda w, blk, e, v, ne, nx: (e[w], 0, 0)
    return pl.pallas_call(
        _gmm_kernel,
        grid_spec=pltpu.PrefetchScalarGridSpec(
            num_scalar_prefetch=5,
            grid=(N_ITEMS,),
            in_specs=[
                pl.BlockSpec((TM_GMM * ROW_TILE, LANES), rows),
                pl.BlockSpec(memory_space=pl.ANY),
                pl.BlockSpec((1, 1, 2 * D_EXPERT), per_e),
                pl.BlockSpec(memory_space=pl.ANY),
                pl.BlockSpec((1, 1, D_MODEL), per_e),
            ],
            out_specs=pl.BlockSpec((TM_GMM * ROW_TILE, LANES), lambda w, blk, e, v, ne, nx: (w, 0)),
            scratch_shapes=[pltpu.VMEM((D_MODEL, 2 * D_EXPERT), F32), pltpu.VMEM((D_EXPERT, D_MODEL), F32),
                            pltpu.VMEM((D_MODEL, 2 * D_EXPERT), BF16), pltpu.VMEM((D_EXPERT, D_MODEL), BF16),
                            pltpu.SemaphoreType.DMA((2,))],
        ),
        out_shape=jax.ShapeDtypeStruct((N_PAD_ROWS * ROW_TILE, LANES), F32),
        compiler_params=pltpu.CompilerParams(
            dimension_semantics=("arbitrary",), vmem_limit_bytes=56 * 1024 * 1024),
        name="moe_experts",
    )(item_blk, item_exp, item_valid, item_newe, item_nexte, xs, w_gu, b_gu, w_down, b_down)


def _combine_kernel(dest_ref, ys_ref, x2_ref, gate_ref, g_ref, o_ref, buf0_ref, buf1_ref, sems):
    i = pl.program_id(0)
    last = pl.num_programs(0) - 1
    bufs = (buf0_ref, buf1_ref)

    def copies(step, slot, r):
        return [pltpu.make_async_copy(ys_ref.at[_tile_rows(dest_ref[(step * TM_COMB + r) * TOP_K + k])],
                                      bufs[slot].at[k, _tile_rows(r)],
                                      sems.at[slot]) for k in range(TOP_K)]

    def start(step, slot):
        def fn(r):
            for k, cp in enumerate(copies(step, slot, r)):
                cp.start(priority=k % 2)
        _for_rows(TM_COMB, fn)

    def wait(step, slot):
        def fn(r):
            for cp in copies(step, slot, r):
                cp.wait()
        _for_rows(TM_COMB, fn)

    @pl.when(i == 0)
    def _():
        start(0, 0)

    nxt = jnp.minimum(i + 1, last)
    for slot in range(2):
        @pl.when(i % 2 == slot)
        def _():
            wait(i, slot)
            for r0 in range(0, TM_COMB, COMB_ROWS):
                for r in range(r0, r0 + COMB_ROWS):
                    for k, cp in enumerate(copies(nxt, 1 - slot, r)):
                        cp.start(priority=k % 2)
                rows = slice(r0, r0 + COMB_ROWS)
                gates = gate_ref[rows, :]
                y = x2_ref[rows, :]
                for k in range(TOP_K):
                    y = y + _from_row_tiles(bufs[slot].at[k], COMB_ROWS, r0) * gates[:, k:k + 1]
                o_ref[rows, :] = _rms(y, g_ref[...])

            @pl.when(i == last)
            def _():
                wait(nxt, 1 - slot)


def _combine(dest_flat, ys, x2, gates_t, g):
    tok = lambda i, d: (i, 0)
    return pl.pallas_call(
        _combine_kernel,
        grid_spec=pltpu.PrefetchScalarGridSpec(
            num_scalar_prefetch=1,
            grid=(N_TOK // TM_COMB,),
            in_specs=[
                pl.BlockSpec(memory_space=pl.ANY),
                pl.BlockSpec((TM_COMB, D_MODEL), tok),
                pl.BlockSpec((TM_COMB, TOP_K), tok),
                pl.BlockSpec((1, D_MODEL), lambda i, d: (0, 0)),
            ],
            out_specs=pl.BlockSpec((TM_COMB, D_MODEL), tok),
            scratch_shapes=[pltpu.VMEM((TOP_K, TM_COMB * ROW_TILE, LANES), F32),
                            pltpu.VMEM((TOP_K, TM_COMB * ROW_TILE, LANES), F32),
                            pltpu.SemaphoreType.DMA((2,))],
        ),
        out_shape=jax.ShapeDtypeStruct((N_TOK, D_MODEL), F32),
        compiler_params=pltpu.CompilerParams(
            dimension_semantics=("arbitrary",), vmem_limit_bytes=48 * 1024 * 1024),
        name="moe_combine",
    )(dest_flat, ys, x2, gates_t, g)


def _rotary_lane_table():
    half = ROT_DIM // 2
    inv_freq = ROPE_THETA ** (-jnp.arange(0, ROT_DIM, 2, dtype=F32) / ROT_DIM)
    rest = jnp.zeros((DIFF_HEAD_DIM - ROT_DIM,), F32)
    zh = jnp.zeros((half,), F32)
    oh = jnp.ones((half,), F32)
    reps = LANES // DIFF_HEAD_DIM
    rows = [jnp.tile(jnp.concatenate(r), reps) for r in
            ([inv_freq, inv_freq, rest], [-oh, zh, rest], [zh, oh, rest])]
    return jnp.concatenate([jnp.stack(rows), jnp.zeros((8 - len(rows), LANES), F32)])


def _work_items(counts):
    n_items = (counts + TM_GMM - 1) // TM_GMM
    item_end = jnp.cumsum(n_items)
    total = item_end[-1]
    starts = (item_end - n_items) * TM_GMM
    w = jnp.arange(N_ITEMS, dtype=I32)
    valid = w < total
    wc = jnp.minimum(w, total - 1)
    e = jnp.minimum(jnp.sum((item_end[None, :] <= wc[:, None]).astype(I32), axis=1), N_EXPERTS - 1)
    is_e = e[:, None] == jnp.arange(N_EXPERTS, dtype=I32)[None, :]
    pick = lambda per_expert: jnp.sum(jnp.where(is_e, per_expert[None, :], 0), axis=1)
    new_e = (valid & (e != jnp.concatenate([jnp.full((1,), -1, I32), e[:-1]]))).astype(I32)
    ids = jnp.arange(N_EXPERTS, dtype=I32)
    later = (ids[None, :] > ids[:, None]) & (counts[None, :] > 0)
    next_of = jnp.min(jnp.where(later, ids[None, :], N_EXPERTS), axis=1).astype(I32)
    next_e = jnp.where(valid, pick(next_of), N_EXPERTS).astype(I32)
    pad_start = (starts + counts).astype(I32)
    pad_len = jnp.concatenate([n_items * TM_GMM - counts, total[None]]).astype(I32)
    return wc.astype(I32), e.astype(I32), valid.astype(I32), new_e, next_e, starts.astype(I32), pad_start, pad_len


def kernel(x, positions, mem, attn_norm_g, w_in, w_pool, pool_scale, lambda_q1, lambda_k1, lambda_q2, lambda_k2, subln_g, w_out, xattn_norm_g, mem_norm_g, w_cq, w_ckv, w_co, ffn_norm_g, w_router, b_router, w_gu, b_gu, w_down, b_down, final_norm_g):
    l = 0
    x2d = x.reshape(N_TOK, D_MODEL)
    u_pool, q, k, v = _in_proj(x2d, attn_norm_g[l].reshape(1, D_MODEL), w_in[l].astype(BF16),
                               positions.reshape(N_TOK, 1), _rotary_lane_table())
    y_pool = _pool(u_pool.reshape(BATCH, SEQ, POOL_WIDTH), w_pool[l].astype(BF16),
                   pool_scale[l].reshape(1, POOL_WIDTH))
    lam_vecs = jnp.stack([lambda_q1[l], lambda_k1[l], lambda_q2[l], lambda_k2[l]]).astype(F32)
    y_diff = _diff_attn(lam_vecs, q.reshape(BATCH, SEQ, QK_WIDTH), k.reshape(BATCH, SEQ, QK_WIDTH),
                        v.reshape(BATCH, SEQ, DIFF_WIDTH), subln_g[l].reshape(1, DIFF_V_DIM))
    kx, vx = _mem_kv(mem, mem_norm_g[l].reshape(1, D_MODEL), w_ckv[l].astype(BF16))

    wr_t = w_router[l].T
    wr_hi = wr_t.astype(BF16)
    wr_lo = (wr_t - wr_hi.astype(F32)).astype(BF16)
    tri = (jnp.arange(TM_MID)[:, None] < jnp.arange(TM_MID)[None, :]).astype(BF16)
    x2, xn, top_idx, gates, rank, cnt = _mid(
        x2d, y_pool.reshape(N_TOK, POOL_WIDTH), y_diff.reshape(N_TOK, DIFF_WIDTH), w_out[l].astype(BF16),
        xattn_norm_g[l].reshape(1, D_MODEL), w_cq[l].astype(BF16), kx, vx, w_co[l].astype(BF16),
        ffn_norm_g[l].reshape(1, D_MODEL), wr_hi, wr_lo, b_router[l].reshape(N_EXPERTS, 1), tri)

    counts = cnt[:, 0].astype(I32)
    item_blk, item_exp, item_valid, item_newe, item_nexte, starts, pad_start, pad_len = _work_items(counts)
    hot = top_idx[:, :, None] == jnp.arange(N_EXPERTS, dtype=I32)
    dest = jnp.sum(jnp.where(hot, starts, 0), axis=-1) + rank
    dest_flat = dest.T.reshape(N_ROWS)

    xs = _dispatch(dest_flat, pad_start, pad_len, xn)
    ys = _gmm(item_blk, item_exp, item_valid, item_newe, item_nexte, xs,
              w_gu[l], b_gu[l].reshape(N_EXPERTS, 1, 2 * D_EXPERT),
              w_down[l], b_down[l].reshape(N_EXPERTS, 1, D_MODEL))
    out = _combine(dest_flat, ys, x2, gates.T, final_norm_g.reshape(1, D_MODEL))
    return out.reshape(BATCH, SEQ, D_MODEL)
```

```python
import functools
import math

import jax
import jax.numpy as jnp
from jax import lax
from jax.experimental import pallas as pl
from jax.experimental.pallas import tpu as pltpu

F32 = jnp.float32
BF16 = jnp.bfloat16
I32 = jnp.int32

D_MODEL = 1024
BATCH = 8
SEQ = 2048
N_TOK = BATCH * SEQ
CHUNK = 64
NORM_EPS = 1e-5
POOL_WIDTH = 512
POOL_WINDOWS = (2, 4, 8, 16)
POOL_GROUP_DIM = 128
MAX_WINDOW = max(POOL_WINDOWS)
DIFF_HEADS = 4
DIFF_HEAD_DIM = 64
DIFF_V_DIM = 128
DIFF_WIDTH = 512
QK_WIDTH = 512
IN_WIDTH = 2048
ROT_DIM = 16
ROPE_THETA = 500000.0
MEM_LEN = 256
X_HEADS = 4
X_HEAD_DIM = 256
N_EXPERTS = 32
TOP_K = 4
D_EXPERT = 1024
SWIGLU_ALPHA = 1.702
SWIGLU_LIMIT = 7.0
LAM_INIT = 0.8 - 0.6 * math.exp(-0.3 * 0)
N_ROWS = N_TOK * TOP_K

LANES = 128

TM_IN = 512
IN_ROWS = 512
TQ = 256
POOL_ROWS = 512
TM_MID = 512
MID_ROWS = 512
TM_DISP = 256
TM_GMM = 512
GMM_ROWS = 256
TM_COMB = 256
COMB_ROWS = 32
ROW_UNROLL = 8
ROW_TILE = D_MODEL // LANES
N_PAD_ROWS = N_ROWS + N_EXPERTS * TM_GMM
N_ITEMS = N_PAD_ROWS // TM_GMM
assert N_EXPERTS <= N_TOK // TM_DISP


def _rms(xf, g):
    ms = jnp.mean(xf * xf, axis=-1, keepdims=True)
    return xf * lax.rsqrt(ms + NORM_EPS) * g


def _dot(a, b):
    return jnp.dot(a, b, preferred_element_type=F32)


def _dot_nt(a, b):
    return lax.dot_general(a, b, (((1,), (1,)), ((), ())), preferred_element_type=F32)


def _in_proj_kernel(x_ref, g_ref, w_ref, pos_ref, rot_ref, up_ref, q_ref, k_ref, v_ref):
    for r0 in range(0, TM_IN, IN_ROWS):
        rows = slice(r0, r0 + IN_ROWS)
        h = _rms(x_ref[rows, :], g_ref[...]).astype(BF16)
        ang = pos_ref[rows, :].astype(F32) * rot_ref[0:1, :]
        cosf = jnp.cos(ang)
        sinf = jnp.sin(ang)
        sa = sinf * rot_ref[1:2, :]
        sb = sinf * rot_ref[2:3, :]
        up_ref[rows, :] = _dot(h, w_ref[:, 0:POOL_WIDTH])
        for off, o_ref, scale in ((POOL_WIDTH, q_ref, DIFF_HEAD_DIM ** -0.5),
                                  (POOL_WIDTH + QK_WIDTH, k_ref, 1.0)):
            u = _dot(h, w_ref[:, off:off + QK_WIDTH])
            for hd in range(DIFF_HEADS):
                uh = u[:, hd * LANES:(hd + 1) * LANES]
                r = uh * cosf + pltpu.roll(uh, LANES - 8, 1) * sa + pltpu.roll(uh, 8, 1) * sb
                o_ref[rows, hd * LANES:(hd + 1) * LANES] = (r * scale).astype(BF16)
        v_ref[rows, :] = _dot(h, w_ref[:, POOL_WIDTH + 2 * QK_WIDTH:IN_WIDTH]).astype(BF16)


def _in_proj(x2d, g, w_in, pos, rot_tab):
    n = N_TOK // TM_IN
    tok = lambda i: (i, 0)
    fixed = lambda i: (0, 0)
    return pl.pallas_call(
        _in_proj_kernel,
        grid=(n,),
        in_specs=[
            pl.BlockSpec((TM_IN, D_MODEL), tok),
            pl.BlockSpec((1, D_MODEL), fixed),
            pl.BlockSpec((D_MODEL, IN_WIDTH), fixed),
            pl.BlockSpec((TM_IN, 1), tok),
            pl.BlockSpec((8, LANES), fixed),
        ],
        out_specs=[
            pl.BlockSpec((TM_IN, POOL_WIDTH), tok),
            pl.BlockSpec((TM_IN, QK_WIDTH), tok),
            pl.BlockSpec((TM_IN, QK_WIDTH), tok),
            pl.BlockSpec((TM_IN, DIFF_WIDTH), tok),
        ],
        out_shape=[
            jax.ShapeDtypeStruct((N_TOK, POOL_WIDTH), F32),
            jax.ShapeDtypeStruct((N_TOK, QK_WIDTH), BF16),
            jax.ShapeDtypeStruct((N_TOK, QK_WIDTH), BF16),
            jax.ShapeDtypeStruct((N_TOK, DIFF_WIDTH), BF16),
        ],
        compiler_params=pltpu.CompilerParams(
            dimension_semantics=("arbitrary",), vmem_limit_bytes=48 * 1024 * 1024),
        name="in_proj",
    )(x2d, g, w_in, pos, rot_tab)


def _pool_kernel(u_ref, w_ref, sc_ref, o_ref, pad_ref):
    pad_ref[0:MAX_WINDOW, :] = jnp.zeros((MAX_WINDOW, POOL_WIDTH), F32)
    pad_ref[MAX_WINDOW:, :] = u_ref[0]
    for g, win in enumerate(POOL_WINDOWS):
        lanes = slice(g * POOL_GROUP_DIM, (g + 1) * POOL_GROUP_DIM)
        for c in range(SEQ // POOL_ROWS):
            r0 = c * POOL_ROWS
            u = pad_ref[MAX_WINDOW + r0:MAX_WINDOW + r0 + POOL_ROWS, lanes]
            acc = u
            for j in range(1, win):
                acc = acc + pad_ref[MAX_WINDOW + r0 - j:MAX_WINDOW + r0 - j + POOL_ROWS, lanes]
            t = r0 + lax.broadcasted_iota(I32, (POOL_ROWS, 1), 0)
            cnt = jnp.minimum(t + 1, win).astype(F32)
            mixed = (acc / cnt - u).astype(BF16)
            y = _dot(mixed, w_ref[g]) * sc_ref[:, lanes]
            o_ref[0, r0:r0 + POOL_ROWS, lanes] = y.astype(BF16)


def _pool(u_pool, w_pool, pool_scale):
    return pl.pallas_call(
        _pool_kernel,
        grid=(BATCH,),
        in_specs=[
            pl.BlockSpec((1, SEQ, POOL_WIDTH), lambda b: (b, 0, 0)),
            pl.BlockSpec((len(POOL_WINDOWS), POOL_GROUP_DIM, POOL_GROUP_DIM), lambda b: (0, 0, 0)),
            pl.BlockSpec((1, POOL_WIDTH), lambda b: (0, 0)),
        ],
        out_specs=pl.BlockSpec((1, SEQ, POOL_WIDTH), lambda b: (b, 0, 0)),
        out_shape=jax.ShapeDtypeStruct((BATCH, SEQ, POOL_WIDTH), BF16),
        scratch_shapes=[pltpu.VMEM((SEQ + MAX_WINDOW, POOL_WIDTH), F32)],
        compiler_params=pltpu.CompilerParams(
            dimension_semantics=("arbitrary",), vmem_limit_bytes=48 * 1024 * 1024),
        name="pool_mixer",
    )(u_pool, w_pool, pool_scale)


def _diff_attn_kernel(lam_ref, q_ref, k_ref, v_ref, g_ref, o_ref):
    lv = lam_ref[...]
    e1 = jnp.exp(jnp.sum(lv[0:1] * lv[1:2], axis=-1, keepdims=True))
    e2 = jnp.exp(jnp.sum(lv[2:3] * lv[3:4], axis=-1, keepdims=True))
    lam = e1 - e2 + LAM_INIT
    lane = lax.broadcasted_iota(I32, (TQ, LANES), 1)
    qc = lax.broadcasted_iota(I32, (TQ, TQ), 0) // CHUNK
    kc = lax.broadcasted_iota(I32, (TQ, TQ), 1) // CHUNK
    diag_mask = kc <= qc
    gain = g_ref[...] * (1.0 - LAM_INIT)
    zero = jnp.zeros((), BF16)
    for qi in range(SEQ // TQ):
        q0 = qi * TQ
        qt = q_ref[0, q0:q0 + TQ, :]
        qmaps = (jnp.where(lane < DIFF_HEAD_DIM, qt, zero), jnp.where(lane >= DIFF_HEAD_DIM, qt, zero))
        kd = k_ref[0, q0:q0 + TQ, :]
        s_diag = [jnp.where(diag_mask, _dot_nt(qm, kd), -jnp.inf) for qm in qmaps]
        if qi > 0:
            ka = k_ref[0, 0:q0, :]
            s_past = [_dot_nt(qm, ka) for qm in qmaps]
        p_diag, p_past, inv = [], [], []
        for m in range(2):
            mx = jnp.max(s_diag[m], axis=-1, keepdims=True)
            if qi > 0:
                mx = jnp.maximum(mx, jnp.max(s_past[m], axis=-1, keepdims=True))
            pd = jnp.exp(s_diag[m] - mx)
            den = jnp.sum(pd, axis=-1, keepdims=True)
            p_diag.append(pd)
            if qi > 0:
                pp = jnp.exp(s_past[m] - mx)
                den = den + jnp.sum(pp, axis=-1, keepdims=True)
                p_past.append(pp)
            inv.append(1.0 / den)
        c1 = inv[0]
        c2 = lam * inv[1]
        a_diag = (p_diag[0] * c1 - p_diag[1] * c2).astype(BF16)
        o = _dot(a_diag, v_ref[0, q0:q0 + TQ, :])
        if qi > 0:
            a_past = (p_past[0] * c1 - p_past[1] * c2).astype(BF16)
            o = o + _dot(a_past, v_ref[0, 0:q0, :])
        o_ref[0, q0:q0 + TQ, :] = _rms(o, gain).astype(BF16)


def _diff_attn(lam_vecs, q, k, v, subln_g):
    blk = pl.BlockSpec((1, SEQ, LANES), lambda b, h: (b, 0, h))
    return pl.pallas_call(
        _diff_attn_kernel,
        grid=(BATCH, DIFF_HEADS),
        in_specs=[
            pl.BlockSpec((4, DIFF_HEAD_DIM), lambda b, h: (0, 0)),
            blk, blk, blk,
            pl.BlockSpec((1, DIFF_V_DIM), lambda b, h: (0, 0)),
        ],
        out_specs=blk,
        out_shape=jax.ShapeDtypeStruct((BATCH, SEQ, DIFF_WIDTH), BF16),
        compiler_params=pltpu.CompilerParams(
            dimension_semantics=("arbitrary", "arbitrary"), vmem_limit_bytes=48 * 1024 * 1024),
        name="diff_attn",
    )(lam_vecs, q, k, v, subln_g)


def _mem_kv_kernel(m_ref, g_ref, w_ref, k_ref, v_ref):
    h = _rms(m_ref[0], g_ref[...]).astype(BF16)
    k_ref[0] = _dot(h, w_ref[:, 0:D_MODEL]).astype(BF16)
    v_ref[0] = _dot(h, w_ref[:, D_MODEL:2 * D_MODEL]).astype(BF16)


def _mem_kv(mem, g, w_ckv):
    blk = pl.BlockSpec((1, MEM_LEN, D_MODEL), lambda b: (b, 0, 0))
    return pl.pallas_call(
        _mem_kv_kernel,
        grid=(BATCH,),
        in_specs=[blk, pl.BlockSpec((1, D_MODEL), lambda b: (0, 0)),
                  pl.BlockSpec((D_MODEL, 2 * D_MODEL), lambda b: (0, 0))],
        out_specs=[blk, blk],
        out_shape=[jax.ShapeDtypeStruct((BATCH, MEM_LEN, D_MODEL), BF16)] * 2,
        compiler_params=pltpu.CompilerParams(
            dimension_semantics=("arbitrary",), vmem_limit_bytes=48 * 1024 * 1024),
        name="mem_kv",
    )(mem, g, w_ckv)


def _mid_kernel(x_ref, yp_ref, yd_ref, wo_ref, g2_ref, wcq_ref, kx_ref, vx_ref, wco_ref, g3_ref,
                wrh_ref, wrl_ref, br_ref, tri_ref,
                x2_ref, xn_ref, idx_ref, gate_ref, rank_ref, cnt_ref, attn_ref, base_ref):
    @pl.when(pl.program_id(0) == 0)
    def _():
        base_ref[...] = jnp.zeros_like(base_ref)

    xn_groups = []
    for r0 in range(0, TM_MID, MID_ROWS):
        rows = slice(r0, r0 + MID_ROWS)
        x1 = (x_ref[rows, :] + _dot(yp_ref[rows, :], wo_ref[0:POOL_WIDTH, :])
              + _dot(yd_ref[rows, :], wo_ref[POOL_WIDTH:, :]))
        h2 = _rms(x1, g2_ref[...]).astype(BF16)
        qx = (_dot(h2, wcq_ref[...]) * (X_HEAD_DIM ** -0.5)).astype(BF16)
        for hd in range(X_HEADS):
            cols = slice(hd * X_HEAD_DIM, (hd + 1) * X_HEAD_DIM)
            s = _dot_nt(qx[:, cols], kx_ref[0, :, cols])
            p = jnp.exp(s - jnp.max(s, axis=-1, keepdims=True))
            p = p / jnp.sum(p, axis=-1, keepdims=True)
            attn_ref[rows, cols] = _dot(p.astype(BF16), vx_ref[0, :, cols]).astype(BF16)
        x2 = x1 + _dot(attn_ref[rows, :], wco_ref[...])
        x2_ref[rows, :] = x2
        xn_g = _rms(x2, g3_ref[...])
        xn_ref[rows, :] = xn_g
        xn_groups.append(xn_g)
    xn = jnp.concatenate(xn_groups, axis=0)

    xh = xn.astype(BF16)
    xl = (xn - xh.astype(F32)).astype(BF16)
    logits = (_dot_nt(wrh_ref[...], xh) + _dot_nt(wrh_ref[...], xl) + _dot_nt(wrl_ref[...], xh)
              + br_ref[...])
    eidx = lax.broadcasted_iota(I32, (N_EXPERTS, TM_MID), 0).astype(F32)
    vals, hots = [], []
    for k in range(TOP_K):
        mx = jnp.max(logits, axis=0, keepdims=True)
        sel = jnp.min(jnp.where(logits == mx, eidx, float(N_EXPERTS)), axis=0, keepdims=True)
        hot = eidx == sel
        idx_ref[k:k + 1, :] = sel.astype(I32)
        vals.append(mx)
        hots.append(hot)
        logits = jnp.where(hot, -jnp.inf, logits)
    ex = [jnp.exp(v - vals[0]) for v in vals]
    den = ex[0] + ex[1] + ex[2] + ex[3]
    for k in range(TOP_K):
        gate_ref[k:k + 1, :] = ex[k] / den

    chosen = jnp.zeros((N_EXPERTS, TM_MID), F32)
    for k in range(TOP_K):
        chosen = chosen + jnp.where(hots[k], 1.0, 0.0)
    before = _dot(chosen.astype(BF16), tri_ref[...]) + base_ref[:, 0:1]
    for k in range(TOP_K):
        rank_ref[k:k + 1, :] = jnp.sum(jnp.where(hots[k], before, 0.0), axis=0, keepdims=True).astype(I32)
    base_ref[...] = base_ref[...] + jnp.sum(chosen, axis=1, keepdims=True)
    cnt_ref[...] = base_ref[...]


def _mid(x2d, y_pool, y_diff, w_out, g2, w_cq, kx, vx, w_co, g3, wr_hi, wr_lo, b_r, tri):
    n = N_TOK // TM_MID
    per_batch = SEQ // TM_MID
    tok = lambda i: (i, 0)
    fixed = lambda i: (0, 0)
    kv = pl.BlockSpec((1, MEM_LEN, D_MODEL), lambda i: (i // per_batch, 0, 0))
    row4 = pl.BlockSpec((TOP_K, TM_MID), lambda i: (0, i))
    return pl.pallas_call(
        _mid_kernel,
        grid=(n,),
        in_specs=[
            pl.BlockSpec((TM_MID, D_MODEL), tok),
            pl.BlockSpec((TM_MID, POOL_WIDTH), tok),
            pl.BlockSpec((TM_MID, DIFF_WIDTH), tok),
            pl.BlockSpec((D_MODEL, D_MODEL), fixed),
            pl.BlockSpec((1, D_MODEL), fixed),
            pl.BlockSpec((D_MODEL, D_MODEL), fixed),
            kv, kv,
            pl.BlockSpec((D_MODEL, D_MODEL), fixed),
            pl.BlockSpec((1, D_MODEL), fixed),
            pl.BlockSpec((N_EXPERTS, D_MODEL), fixed),
            pl.BlockSpec((N_EXPERTS, D_MODEL), fixed),
            pl.BlockSpec((N_EXPERTS, 1), fixed),
            pl.BlockSpec((TM_MID, TM_MID), fixed),
        ],
        out_specs=[
            pl.BlockSpec((TM_MID, D_MODEL), tok),
            pl.BlockSpec((TM_MID, D_MODEL), tok),
            row4, row4, row4,
            pl.BlockSpec((N_EXPERTS, LANES), fixed),
        ],
        out_shape=[
            jax.ShapeDtypeStruct((N_TOK, D_MODEL), F32),
            jax.ShapeDtypeStruct((N_TOK, D_MODEL), F32),
            jax.ShapeDtypeStruct((TOP_K, N_TOK), I32),
            jax.ShapeDtypeStruct((TOP_K, N_TOK), F32),
            jax.ShapeDtypeStruct((TOP_K, N_TOK), I32),
            jax.ShapeDtypeStruct((N_EXPERTS, LANES), F32),
        ],
        scratch_shapes=[pltpu.VMEM((TM_MID, D_MODEL), BF16), pltpu.VMEM((N_EXPERTS, LANES), F32)],
        compiler_params=pltpu.CompilerParams(
            dimension_semantics=("arbitrary",), vmem_limit_bytes=56 * 1024 * 1024),
        name="mid_block",
    )(x2d, y_pool, y_diff, w_out, g2, w_cq, kx, vx, w_co, g3, wr_hi, wr_lo, b_r, tri)


def _for_rows(n_rows, fn):
    def body(g, c):
        for j in range(ROW_UNROLL):
            fn(g * ROW_UNROLL + j)
        return c

    lax.fori_loop(0, n_rows // ROW_UNROLL, body, 0)


def _to_row_tiles(dst_ref, val, row0=0):
    n = val.shape[0]
    for s in range(ROW_TILE):
        dst_ref[pl.ds(row0 * ROW_TILE + s, n, stride=ROW_TILE), :] = val[:, s * LANES:(s + 1) * LANES]


def _from_row_tiles(src_ref, n, row0=0):
    return jnp.concatenate(
        [src_ref[pl.ds(row0 * ROW_TILE + s, n, stride=ROW_TILE), :] for s in range(ROW_TILE)], axis=1)


def _tile_rows(row):
    return pl.ds(pl.multiple_of(row * ROW_TILE, ROW_TILE), ROW_TILE)


def _dispatch_kernel(dest_ref, pad_start_ref, pad_len_ref, xn_ref, xs_ref, stage_ref, zero_ref, sems, pad_sem):
    i = pl.program_id(0)
    last = pl.num_programs(0) - 1

    def expert_pad_copies(e, visit):
        row = pad_start_ref[e]
        n = pad_len_ref[e]
        for bit in reversed(range(TM_GMM.bit_length() - 1)):
            size = 1 << bit
            take = (n & size) != 0
            offset = n & ~(2 * size - 1)

            @pl.when(take)
            def _():
                visit(pltpu.make_async_copy(
                    zero_ref.at[pl.ds(0, size * ROW_TILE)],
                    xs_ref.at[pl.ds(pl.multiple_of((row + offset) * ROW_TILE, ROW_TILE), size * ROW_TILE)],
                    pad_sem))

    def unused_block_copies(b, visit):
        @pl.when((b < N_ITEMS) & (b >= pad_len_ref[N_EXPERTS]))
        def _():
            for half in range(2):
                rows = zero_ref.shape[0]
                visit(pltpu.make_async_copy(
                    zero_ref,
                    xs_ref.at[pl.ds(pl.multiple_of(b * TM_GMM * ROW_TILE + half * rows, rows), rows)],
                    pad_sem))

    n_steps = N_TOK // TM_DISP
    blocks_per_step = -(-N_ITEMS // n_steps)

    def start_pad(step):
        @pl.when(step < N_EXPERTS)
        def _():
            expert_pad_copies(step, lambda cp: cp.start())
        for m in range(blocks_per_step):
            unused_block_copies(step + m * n_steps, lambda cp: cp.start())

    def wait_all_pad():
        def per_expert(e, c):
            expert_pad_copies(e, lambda cp: cp.wait())
            return c
        lax.fori_loop(0, N_EXPERTS, per_expert, 0)

        def per_block(b, c):
            unused_block_copies(b, lambda cp: cp.wait())
            return c
        lax.fori_loop(0, N_ITEMS, per_block, 0)

    @pl.when(i == 0)
    def _():
        zero_ref[...] = jnp.zeros_like(zero_ref)

    def copies(step, slot, r):
        return [pltpu.make_async_copy(stage_ref.at[slot, _tile_rows(r)],
                                      xs_ref.at[_tile_rows(dest_ref[k * N_TOK + step * TM_DISP + r])],
                                      sems.at[slot]) for k in range(TOP_K)]

    def start(step, slot):
        def fn(r):
            for k, cp in enumerate(copies(step, slot, r)):
                cp.start(priority=k % 2)
        _for_rows(TM_DISP, fn)

    def wait(step, slot):
        def fn(r):
            for cp in copies(step, slot, r):
                cp.wait()
        _for_rows(TM_DISP, fn)

    for slot in range(2):
        @pl.when(i % 2 == slot)
        def _():
            _to_row_tiles(stage_ref.at[slot], xn_ref[...])
            start(i, slot)

            @pl.when(i > 0)
            def _():
                wait(i - 1, 1 - slot)

            @pl.when(i == last)
            def _():
                wait(i, slot)

    start_pad(i)

    @pl.when(i == last)
    def _():
        wait_all_pad()


def _dispatch(dest_flat, pad_start, pad_len, xn):
    return pl.pallas_call(
        _dispatch_kernel,
        grid_spec=pltpu.PrefetchScalarGridSpec(
            num_scalar_prefetch=3,
            grid=(N_TOK // TM_DISP,),
            in_specs=[pl.BlockSpec((TM_DISP, D_MODEL), lambda i, d, ps, pn: (i, 0))],
            out_specs=pl.BlockSpec(memory_space=pl.ANY),
            scratch_shapes=[pltpu.VMEM((2, TM_DISP * ROW_TILE, LANES), F32),
                            pltpu.VMEM((TM_GMM // 2 * ROW_TILE, LANES), F32),
                            pltpu.SemaphoreType.DMA((2,)), pltpu.SemaphoreType.DMA(())],
        ),
        out_shape=jax.ShapeDtypeStruct((N_PAD_ROWS * ROW_TILE, LANES), F32),
        compiler_params=pltpu.CompilerParams(dimension_semantics=("arbitrary",)),
        name="moe_dispatch",
    )(dest_flat, pad_start, pad_len, xn)


def _gmm_kernel(blk_ref, exp_ref, groups_ref, newe_ref, nexte_ref, xs_ref, wgu_ref, bgu_ref, wd_ref,
                bd_ref, o_ref, wgu_f, wd_f, wgu_s, wd_s, sems):
    w = pl.program_id(0)

    def weight_copies(e):
        return (pltpu.make_async_copy(wgu_ref.at[e], wgu_f, sems.at[0]),
                pltpu.make_async_copy(wd_ref.at[e], wd_f, sems.at[1]))

    @pl.when(w == 0)
    def _():
        for cp in weight_copies(exp_ref[0]):
            cp.start()

    @pl.when(newe_ref[w] == 1)
    def _():
        for cp in weight_copies(exp_ref[w]):
            cp.wait()
        wgu_s[...] = wgu_f[...].astype(BF16)
        wd_s[...] = wd_f[...].astype(BF16)

        @pl.when(nexte_ref[w] < N_EXPERTS)
        def _():
            for cp in weight_copies(nexte_ref[w]):
                cp.start()

    def expert_ffn(n_groups):
        for r0 in range(0, n_groups * GMM_ROWS, GMM_ROWS):
            xb = _from_row_tiles(xs_ref, GMM_ROWS, r0).astype(BF16)
            gate = _dot(xb, wgu_s[:, 0:D_EXPERT]) + bgu_ref[0, :, 0:D_EXPERT]
            up = _dot(xb, wgu_s[:, D_EXPERT:]) + bgu_ref[0, :, D_EXPERT:]
            gate = jnp.minimum(gate, SWIGLU_LIMIT)
            up = jnp.clip(up, -SWIGLU_LIMIT, SWIGLU_LIMIT)
            hid = (up + 1.0) * gate * jax.nn.sigmoid(SWIGLU_ALPHA * gate)
            y = _dot(hid.astype(BF16), wd_s[...]) + bd_ref[0]
            _to_row_tiles(o_ref, y, r0)
        rest = (TM_GMM // GMM_ROWS - n_groups) * GMM_ROWS * ROW_TILE
        if rest:
            o_ref[pl.ds(n_groups * GMM_ROWS * ROW_TILE, rest), :] = jnp.zeros((rest, LANES), F32)

    for n_groups in range(TM_GMM // GMM_ROWS + 1):
        @pl.when(groups_ref[w] == n_groups)
        def _():
            expert_ffn(n_groups)


def _gmm(item_blk, item_exp, item_groups, item_newe, item_nexte, xs, w_gu, b_gu, w_down, b_down):
    rows = lambda w, blk, e, v, ne, nx: (blk[w], 0)
    per_e = lambda w, blk, e, v, ne, nx: (e[w], 0, 0)
    return pl.pallas_call(
        _gmm_kernel,
        grid_spec=pltpu.PrefetchScalarGridSpec(
            num_scalar_prefetch=5,
            grid=(N_ITEMS,),
            in_specs=[
                pl.BlockSpec((TM_GMM * ROW_TILE, LANES), rows),
                pl.BlockSpec(memory_space=pl.ANY),
                pl.BlockSpec((1, 1, 2 * D_EXPERT), per_e),
                pl.BlockSpec(memory_space=pl.ANY),
                pl.BlockSpec((1, 1, D_MODEL), per_e),
            ],
            out_specs=pl.BlockSpec((TM_GMM * ROW_TILE, LANES), lambda w, blk, e, v, ne, nx: (w, 0)),
            scratch_shapes=[pltpu.VMEM((D_MODEL, 2 * D_EXPERT), F32), pltpu.VMEM((D_EXPERT, D_MODEL), F32),
                            pltpu.VMEM((D_MODEL, 2 * D_EXPERT), BF16), pltpu.VMEM((D_EXPERT, D_MODEL), BF16),
                            pltpu.SemaphoreType.DMA((2,))],
        ),
        out_shape=jax.ShapeDtypeStruct((N_PAD_ROWS * ROW_TILE, LANES), F32),
        compiler_params=pltpu.CompilerParams(
            dimension_semantics=("arbitrary",), vmem_limit_bytes=56 * 1024 * 1024),
        name="moe_experts",
    )(item_blk, item_exp, item_groups, item_newe, item_nexte, xs, w_gu, b_gu, w_down, b_down)


def _combine_kernel(dest_ref, ys_ref, x2_ref, gate_ref, g_ref, o_ref, buf0_ref, buf1_ref, sems):
    i = pl.program_id(0)
    last = pl.num_programs(0) - 1
    bufs = (buf0_ref, buf1_ref)

    def copies(step, slot, r):
        return [pltpu.make_async_copy(ys_ref.at[_tile_rows(dest_ref[k * N_TOK + step * TM_COMB + r])],
                                      bufs[slot].at[k, _tile_rows(r)],
                                      sems.at[slot]) for k in range(TOP_K)]

    def start(step, slot):
        def fn(r):
            for k, cp in enumerate(copies(step, slot, r)):
                cp.start(priority=k % 2)
        _for_rows(TM_COMB, fn)

    def wait(step, slot):
        def fn(r):
            for cp in copies(step, slot, r):
                cp.wait()
        _for_rows(TM_COMB, fn)

    @pl.when(i == 0)
    def _():
        start(0, 0)

    nxt = jnp.minimum(i + 1, last)
    for slot in range(2):
        @pl.when(i % 2 == slot)
        def _():
            wait(i, slot)
            for r0 in range(0, TM_COMB, COMB_ROWS):
                for r in range(r0, r0 + COMB_ROWS):
                    for k, cp in enumerate(copies(nxt, 1 - slot, r)):
                        cp.start(priority=k % 2)
                rows = slice(r0, r0 + COMB_ROWS)
                gates = gate_ref[rows, :]
                y = x2_ref[rows, :]
                for k in range(TOP_K):
                    y = y + _from_row_tiles(bufs[slot].at[k], COMB_ROWS, r0) * gates[:, k:k + 1]
                o_ref[rows, :] = _rms(y, g_ref[...])

            @pl.when(i == last)
            def _():
                wait(nxt, 1 - slot)


def _combine(dest_flat, ys, x2, gates_t, g):
    tok = lambda i, d: (i, 0)
    return pl.pallas_call(
        _combine_kernel,
        grid_spec=pltpu.PrefetchScalarGridSpec(
            num_scalar_prefetch=1,
            grid=(N_TOK // TM_COMB,),
            in_specs=[
                pl.BlockSpec(memory_space=pl.ANY),
                pl.BlockSpec((TM_COMB, D_MODEL), tok),
                pl.BlockSpec((TM_COMB, TOP_K), tok),
                pl.BlockSpec((1, D_MODEL), lambda i, d: (0, 0)),
            ],
            out_specs=pl.BlockSpec((TM_COMB, D_MODEL), tok),
            scratch_shapes=[pltpu.VMEM((TOP_K, TM_COMB * ROW_TILE, LANES), F32),
                            pltpu.VMEM((TOP_K, TM_COMB * ROW_TILE, LANES), F32),
                            pltpu.SemaphoreType.DMA((2,))],
        ),
        out_shape=jax.ShapeDtypeStruct((N_TOK, D_MODEL), F32),
        compiler_params=pltpu.CompilerParams(
            dimension_semantics=("arbitrary",), vmem_limit_bytes=48 * 1024 * 1024),
        name="moe_combine",
    )(dest_flat, ys, x2, gates_t, g)


def _rotary_lane_table():
    half = ROT_DIM // 2
    inv_freq = ROPE_THETA ** (-jnp.arange(0, ROT_DIM, 2, dtype=F32) / ROT_DIM)
    rest = jnp.zeros((DIFF_HEAD_DIM - ROT_DIM,), F32)
    zh = jnp.zeros((half,), F32)
    oh = jnp.ones((half,), F32)
    reps = LANES // DIFF_HEAD_DIM
    rows = [jnp.tile(jnp.concatenate(r), reps) for r in
            ([inv_freq, inv_freq, rest], [-oh, zh, rest], [zh, oh, rest])]
    return jnp.concatenate([jnp.stack(rows), jnp.zeros((8 - len(rows), LANES), F32)])


def _work_items(counts):
    n_items = (counts + TM_GMM - 1) // TM_GMM
    item_end = jnp.cumsum(n_items)
    total = item_end[-1]
    starts = (item_end - n_items) * TM_GMM
    w = jnp.arange(N_ITEMS, dtype=I32)
    valid = w < total
    wc = jnp.minimum(w, total - 1)
    e = jnp.minimum(jnp.sum((item_end[None, :] <= wc[:, None]).astype(I32), axis=1), N_EXPERTS - 1)
    is_e = e[:, None] == jnp.arange(N_EXPERTS, dtype=I32)[None, :]
    pick = lambda per_expert: jnp.sum(jnp.where(is_e, per_expert[None, :], 0), axis=1)
    new_e = (valid & (e != jnp.concatenate([jnp.full((1,), -1, I32), e[:-1]]))).astype(I32)
    ids = jnp.arange(N_EXPERTS, dtype=I32)
    later = (ids[None, :] > ids[:, None]) & (counts[None, :] > 0)
    next_of = jnp.min(jnp.where(later, ids[None, :], N_EXPERTS), axis=1).astype(I32)
    next_e = jnp.where(valid, pick(next_of), N_EXPERTS).astype(I32)
    pad_start = (starts + counts).astype(I32)
    pad_len = jnp.concatenate([n_items * TM_GMM - counts, total[None]]).astype(I32)
    real_rows = jnp.clip(pick(counts) - (w - pick(item_end - n_items)) * TM_GMM, 0, TM_GMM)
    groups = jnp.where(valid, (real_rows + GMM_ROWS - 1) // GMM_ROWS, 0).astype(I32)
    return wc.astype(I32), e.astype(I32), groups, new_e, next_e, starts.astype(I32), pad_start, pad_len


def kernel(x, positions, mem, attn_norm_g, w_in, w_pool, pool_scale, lambda_q1, lambda_k1, lambda_q2, lambda_k2, subln_g, w_out, xattn_norm_g, mem_norm_g, w_cq, w_ckv, w_co, ffn_norm_g, w_router, b_router, w_gu, b_gu, w_down, b_down, final_norm_g):
    l = 0
    x2d = x.reshape(N_TOK, D_MODEL)
    u_pool, q, k, v = _in_proj(x2d, attn_norm_g[l].reshape(1, D_MODEL), w_in[l].astype(BF16),
                               positions.reshape(N_TOK, 1), _rotary_lane_table())
    y_pool = _pool(u_pool.reshape(BATCH, SEQ, POOL_WIDTH), w_pool[l].astype(BF16),
                   pool_scale[l].reshape(1, POOL_WIDTH))
    lam_vecs = jnp.stack([lambda_q1[l], lambda_k1[l], lambda_q2[l], lambda_k2[l]]).astype(F32)
    y_diff = _diff_attn(lam_vecs, q.reshape(BATCH, SEQ, QK_WIDTH), k.reshape(BATCH, SEQ, QK_WIDTH),
                        v.reshape(BATCH, SEQ, DIFF_WIDTH), subln_g[l].reshape(1, DIFF_V_DIM))
    kx, vx = _mem_kv(mem, mem_norm_g[l].reshape(1, D_MODEL), w_ckv[l].astype(BF16))

    wr_t = w_router[l].T
    wr_hi = wr_t.astype(BF16)
    wr_lo = (wr_t - wr_hi.astype(F32)).astype(BF16)
    tri = (jnp.arange(TM_MID)[:, None] < jnp.arange(TM_MID)[None, :]).astype(BF16)
    x2, xn, top_idx, gates, rank, cnt = _mid(
        x2d, y_pool.reshape(N_TOK, POOL_WIDTH), y_diff.reshape(N_TOK, DIFF_WIDTH), w_out[l].astype(BF16),
        xattn_norm_g[l].reshape(1, D_MODEL), w_cq[l].astype(BF16), kx, vx, w_co[l].astype(BF16),
        ffn_norm_g[l].reshape(1, D_MODEL), wr_hi, wr_lo, b_router[l].reshape(N_EXPERTS, 1), tri)

    counts = cnt[:, 0].astype(I32)
    item_blk, item_exp, item_groups, item_newe, item_nexte, starts, pad_start, pad_len = _work_items(counts)
    hot = top_idx[:, :, None] == jnp.arange(N_EXPERTS, dtype=I32)
    dest = jnp.sum(jnp.where(hot, starts, 0), axis=-1) + rank
    dest_flat = dest.reshape(N_ROWS)

    xs = _dispatch(dest_flat, pad_start, pad_len, xn)
    ys = _gmm(item_blk, item_exp, item_groups, item_newe, item_nexte, xs,
              w_gu[l], b_gu[l].reshape(N_EXPERTS, 1, 2 * D_EXPERT),
              w_down[l], b_down[l].reshape(N_EXPERTS, 1, D_MODEL))
    out = _combine(dest_flat, ys, x2, gates.T, final_norm_g.reshape(1, D_MODEL))
    return out.reshape(BATCH, SEQ, D_MODEL)
```

```python
import functools
import math

import jax
import jax.numpy as jnp
from jax import lax
from jax.experimental import pallas as pl
from jax.experimental.pallas import tpu as pltpu

F32 = jnp.float32
BF16 = jnp.bfloat16
I32 = jnp.int32

D_MODEL = 1024
BATCH = 8
SEQ = 2048
N_TOK = BATCH * SEQ
CHUNK = 64
NORM_EPS = 1e-5
POOL_WIDTH = 512
POOL_WINDOWS = (2, 4, 8, 16)
POOL_GROUP_DIM = 128
MAX_WINDOW = max(POOL_WINDOWS)
DIFF_HEADS = 4
DIFF_HEAD_DIM = 64
DIFF_V_DIM = 128
DIFF_WIDTH = 512
QK_WIDTH = 512
IN_WIDTH = 2048
ROT_DIM = 16
ROPE_THETA = 500000.0
MEM_LEN = 256
X_HEADS = 4
X_HEAD_DIM = 256
N_EXPERTS = 32
TOP_K = 4
D_EXPERT = 1024
SWIGLU_ALPHA = 1.702
SWIGLU_LIMIT = 7.0
LAM_INIT = 0.8 - 0.6 * math.exp(-0.3 * 0)
N_ROWS = N_TOK * TOP_K

LANES = 128

TM_IN = 512
IN_ROWS = 512
TQ = 256
POOL_ROWS = 512
TM_MID = 512
MID_ROWS = 256
TM_DISP = 256
TM_GMM = 512
GMM_ROWS = 256
TM_COMB = 256
COMB_ROWS = 32
ROW_UNROLL = 8
ROW_TILE = D_MODEL // LANES
N_PAD_ROWS = N_ROWS + N_EXPERTS * TM_GMM
N_ITEMS = N_PAD_ROWS // TM_GMM
assert N_EXPERTS <= N_TOK // TM_DISP


def _rms(xf, g):
    ms = jnp.mean(xf * xf, axis=-1, keepdims=True)
    return xf * lax.rsqrt(ms + NORM_EPS) * g


def _dot(a, b):
    return jnp.dot(a, b, preferred_element_type=F32)


def _dot_nt(a, b):
    return lax.dot_general(a, b, (((1,), (1,)), ((), ())), preferred_element_type=F32)


def _in_proj_kernel(x_ref, g_ref, w_ref, pos_ref, rot_ref, up_ref, q_ref, k_ref, v_ref):
    for r0 in range(0, TM_IN, IN_ROWS):
        rows = slice(r0, r0 + IN_ROWS)
        h = _rms(x_ref[rows, :], g_ref[...]).astype(BF16)
        ang = pos_ref[rows, :].astype(F32) * rot_ref[0:1, :]
        cosf = jnp.cos(ang)
        sinf = jnp.sin(ang)
        sa = sinf * rot_ref[1:2, :]
        sb = sinf * rot_ref[2:3, :]
        up_ref[rows, :] = _dot(h, w_ref[:, 0:POOL_WIDTH])
        for off, o_ref, scale in ((POOL_WIDTH, q_ref, DIFF_HEAD_DIM ** -0.5),
                                  (POOL_WIDTH + QK_WIDTH, k_ref, 1.0)):
            u = _dot(h, w_ref[:, off:off + QK_WIDTH])
            for hd in range(DIFF_HEADS):
                uh = u[:, hd * LANES:(hd + 1) * LANES]
                r = uh * cosf + pltpu.roll(uh, LANES - 8, 1) * sa + pltpu.roll(uh, 8, 1) * sb
                o_ref[rows, hd * LANES:(hd + 1) * LANES] = (r * scale).astype(BF16)
        v_ref[rows, :] = _dot(h, w_ref[:, POOL_WIDTH + 2 * QK_WIDTH:IN_WIDTH]).astype(BF16)


def _in_proj(x2d, g, w_in, pos, rot_tab):
    n = N_TOK // TM_IN
    tok = lambda i: (i, 0)
    fixed = lambda i: (0, 0)
    return pl.pallas_call(
        _in_proj_kernel,
        grid=(n,),
        in_specs=[
            pl.BlockSpec((TM_IN, D_MODEL), tok),
            pl.BlockSpec((1, D_MODEL), fixed),
            pl.BlockSpec((D_MODEL, IN_WIDTH), fixed),
            pl.BlockSpec((TM_IN, 1), tok),
            pl.BlockSpec((8, LANES), fixed),
        ],
        out_specs=[
            pl.BlockSpec((TM_IN, POOL_WIDTH), tok),
            pl.BlockSpec((TM_IN, QK_WIDTH), tok),
            pl.BlockSpec((TM_IN, QK_WIDTH), tok),
            pl.BlockSpec((TM_IN, DIFF_WIDTH), tok),
        ],
        out_shape=[
            jax.ShapeDtypeStruct((N_TOK, POOL_WIDTH), F32),
            jax.ShapeDtypeStruct((N_TOK, QK_WIDTH), BF16),
            jax.ShapeDtypeStruct((N_TOK, QK_WIDTH), BF16),
            jax.ShapeDtypeStruct((N_TOK, DIFF_WIDTH), BF16),
        ],
        compiler_params=pltpu.CompilerParams(
            dimension_semantics=("arbitrary",), vmem_limit_bytes=48 * 1024 * 1024),
        name="in_proj",
    )(x2d, g, w_in, pos, rot_tab)


def _pool_kernel(u_ref, w_ref, sc_ref, o_ref, pad_ref):
    pad_ref[0:MAX_WINDOW, :] = jnp.zeros((MAX_WINDOW, POOL_WIDTH), F32)
    pad_ref[MAX_WINDOW:, :] = u_ref[0]
    for g, win in enumerate(POOL_WINDOWS):
        lanes = slice(g * POOL_GROUP_DIM, (g + 1) * POOL_GROUP_DIM)
        for c in range(SEQ // POOL_ROWS):
            r0 = c * POOL_ROWS
            u = pad_ref[MAX_WINDOW + r0:MAX_WINDOW + r0 + POOL_ROWS, lanes]
            acc = u
            for j in range(1, win):
                acc = acc + pad_ref[MAX_WINDOW + r0 - j:MAX_WINDOW + r0 - j + POOL_ROWS, lanes]
            t = r0 + lax.broadcasted_iota(I32, (POOL_ROWS, 1), 0)
            cnt = jnp.minimum(t + 1, win).astype(F32)
            mixed = (acc / cnt - u).astype(BF16)
            y = _dot(mixed, w_ref[g]) * sc_ref[:, lanes]
            o_ref[0, r0:r0 + POOL_ROWS, lanes] = y.astype(BF16)


def _pool(u_pool, w_pool, pool_scale):
    return pl.pallas_call(
        _pool_kernel,
        grid=(BATCH,),
        in_specs=[
            pl.BlockSpec((1, SEQ, POOL_WIDTH), lambda b: (b, 0, 0)),
            pl.BlockSpec((len(POOL_WINDOWS), POOL_GROUP_DIM, POOL_GROUP_DIM), lambda b: (0, 0, 0)),
            pl.BlockSpec((1, POOL_WIDTH), lambda b: (0, 0)),
        ],
        out_specs=pl.BlockSpec((1, SEQ, POOL_WIDTH), lambda b: (b, 0, 0)),
        out_shape=jax.ShapeDtypeStruct((BATCH, SEQ, POOL_WIDTH), BF16),
        scratch_shapes=[pltpu.VMEM((SEQ + MAX_WINDOW, POOL_WIDTH), F32)],
        compiler_params=pltpu.CompilerParams(
            dimension_semantics=("arbitrary",), vmem_limit_bytes=48 * 1024 * 1024),
        name="pool_mixer",
    )(u_pool, w_pool, pool_scale)


def _diff_attn_kernel(lam_ref, q_ref, k_ref, v_ref, g_ref, o_ref):
    lv = lam_ref[...]
    e1 = jnp.exp(jnp.sum(lv[0:1] * lv[1:2], axis=-1, keepdims=True))
    e2 = jnp.exp(jnp.sum(lv[2:3] * lv[3:4], axis=-1, keepdims=True))
    lam = e1 - e2 + LAM_INIT
    lane = lax.broadcasted_iota(I32, (TQ, LANES), 1)
    qc = lax.broadcasted_iota(I32, (TQ, TQ), 0) // CHUNK
    kc = lax.broadcasted_iota(I32, (TQ, TQ), 1) // CHUNK
    diag_mask = kc <= qc
    gain = g_ref[...] * (1.0 - LAM_INIT)
    zero = jnp.zeros((), BF16)
    def scores(qi):
        q0 = qi * TQ
        qt = q_ref[0, q0:q0 + TQ, :]
        qmaps = (jnp.where(lane < DIFF_HEAD_DIM, qt, zero), jnp.where(lane >= DIFF_HEAD_DIM, qt, zero))
        kd = k_ref[0, q0:q0 + TQ, :]
        s_diag = [jnp.where(diag_mask, _dot_nt(qm, kd), -jnp.inf) for qm in qmaps]
        s_past = [_dot_nt(qm, k_ref[0, 0:q0, :]) for qm in qmaps] if qi > 0 else None
        return s_diag, s_past

    def weights(qi, s_diag, s_past):
        p_diag, p_past, inv = [], [], []
        for m in range(2):
            mx = jnp.max(s_diag[m], axis=-1, keepdims=True)
            if qi > 0:
                mx = jnp.maximum(mx, jnp.max(s_past[m], axis=-1, keepdims=True))
            pd = jnp.exp(s_diag[m] - mx)
            den = jnp.sum(pd, axis=-1, keepdims=True)
            p_diag.append(pd)
            if qi > 0:
                pp = jnp.exp(s_past[m] - mx)
                den = den + jnp.sum(pp, axis=-1, keepdims=True)
                p_past.append(pp)
            inv.append(1.0 / den)
        c1 = inv[0]
        c2 = lam * inv[1]
        a_diag = (p_diag[0] * c1 - p_diag[1] * c2).astype(BF16)
        a_past = (p_past[0] * c1 - p_past[1] * c2).astype(BF16) if qi > 0 else None
        return a_diag, a_past

    def output(qi, a_diag, a_past):
        q0 = qi * TQ
        o = _dot(a_diag, v_ref[0, q0:q0 + TQ, :])
        if qi > 0:
            o = o + _dot(a_past, v_ref[0, 0:q0, :])
        o_ref[0, q0:q0 + TQ, :] = _rms(o, gain).astype(BF16)

    n_tiles = SEQ // TQ
    s_ahead = scores(0)
    a_behind = None
    for qi in range(n_tiles):
        s_now = s_ahead
        if qi + 1 < n_tiles:
            s_ahead = scores(qi + 1)
        a_now = weights(qi, *s_now)
        if a_behind is not None:
            output(qi - 1, *a_behind)
        a_behind = a_now
    output(n_tiles - 1, *a_behind)


def _diff_attn(lam_vecs, q, k, v, subln_g):
    blk = pl.BlockSpec((1, SEQ, LANES), lambda b, h: (b, 0, h))
    return pl.pallas_call(
        _diff_attn_kernel,
        grid=(BATCH, DIFF_HEADS),
        in_specs=[
            pl.BlockSpec((4, DIFF_HEAD_DIM), lambda b, h: (0, 0)),
            blk, blk, blk,
            pl.BlockSpec((1, DIFF_V_DIM), lambda b, h: (0, 0)),
        ],
        out_specs=blk,
        out_shape=jax.ShapeDtypeStruct((BATCH, SEQ, DIFF_WIDTH), BF16),
        compiler_params=pltpu.CompilerParams(
            dimension_semantics=("arbitrary", "arbitrary"), vmem_limit_bytes=48 * 1024 * 1024),
        name="diff_attn",
    )(lam_vecs, q, k, v, subln_g)


def _mem_kv_kernel(m_ref, g_ref, w_ref, k_ref, v_ref):
    h = _rms(m_ref[0], g_ref[...]).astype(BF16)
    k_ref[0] = _dot(h, w_ref[:, 0:D_MODEL]).astype(BF16)
    v_ref[0] = _dot(h, w_ref[:, D_MODEL:2 * D_MODEL]).astype(BF16)


def _mem_kv(mem, g, w_ckv):
    blk = pl.BlockSpec((1, MEM_LEN, D_MODEL), lambda b: (b, 0, 0))
    return pl.pallas_call(
        _mem_kv_kernel,
        grid=(BATCH,),
        in_specs=[blk, pl.BlockSpec((1, D_MODEL), lambda b: (0, 0)),
                  pl.BlockSpec((D_MODEL, 2 * D_MODEL), lambda b: (0, 0))],
        out_specs=[blk, blk],
        out_shape=[jax.ShapeDtypeStruct((BATCH, MEM_LEN, D_MODEL), BF16)] * 2,
        compiler_params=pltpu.CompilerParams(
            dimension_semantics=("arbitrary",), vmem_limit_bytes=48 * 1024 * 1024),
        name="mem_kv",
    )(mem, g, w_ckv)


def _mid_kernel(x_ref, yp_ref, yd_ref, wo_ref, g2_ref, wcq_ref, kx_ref, vx_ref, wco_ref, g3_ref,
                wrh_ref, wrl_ref, br_ref, tri_ref,
                x2_ref, xn_ref, idx_ref, gate_ref, rank_ref, cnt_ref, attn_ref, base_ref):
    @pl.when(pl.program_id(0) == 0)
    def _():
        base_ref[...] = jnp.zeros_like(base_ref)

    groups = [slice(r0, r0 + MID_ROWS) for r0 in range(0, TM_MID, MID_ROWS)]

    def out_proj(rows):
        return (x_ref[rows, :] + _dot(yp_ref[rows, :], wo_ref[0:POOL_WIDTH, :])
                + _dot(yd_ref[rows, :], wo_ref[POOL_WIDTH:, :]))

    def query(x1):
        h2 = _rms(x1, g2_ref[...]).astype(BF16)
        return (_dot(h2, wcq_ref[...]) * (X_HEAD_DIM ** -0.5)).astype(BF16)

    def head_scores(qx, hd):
        cols = slice(hd * X_HEAD_DIM, (hd + 1) * X_HEAD_DIM)
        return _dot_nt(qx[:, cols], kx_ref[0, :, cols])

    def head_output(rows, s, hd):
        cols = slice(hd * X_HEAD_DIM, (hd + 1) * X_HEAD_DIM)
        p = jnp.exp(s - jnp.max(s, axis=-1, keepdims=True))
        p = p / jnp.sum(p, axis=-1, keepdims=True)
        attn_ref[rows, cols] = _dot(p.astype(BF16), vx_ref[0, :, cols]).astype(BF16)

    x1s = [out_proj(rows) for rows in groups]
    qxs = [query(x1) for x1 in x1s]
    tasks = [(g, hd) for hd in range(X_HEADS) for g in range(len(groups))]
    s_ahead = head_scores(qxs[tasks[0][0]], tasks[0][1])
    for t, (g, hd) in enumerate(tasks):
        s_now = s_ahead
        if t + 1 < len(tasks):
            s_ahead = head_scores(qxs[tasks[t + 1][0]], tasks[t + 1][1])
        head_output(groups[g], s_now, hd)
    xn_groups = []
    for rows, x1 in zip(groups, x1s):
        x2 = x1 + _dot(attn_ref[rows, :], wco_ref[...])
        x2_ref[rows, :] = x2
        xn_g = _rms(x2, g3_ref[...])
        xn_ref[rows, :] = xn_g
        xn_groups.append(xn_g)
    xn = jnp.concatenate(xn_groups, axis=0)

    xh = xn.astype(BF16)
    xl = (xn - xh.astype(F32)).astype(BF16)
    logits = (_dot_nt(wrh_ref[...], xh) + _dot_nt(wrh_ref[...], xl) + _dot_nt(wrl_ref[...], xh)
              + br_ref[...])
    eidx = lax.broadcasted_iota(I32, (N_EXPERTS, TM_MID), 0).astype(F32)
    vals, hots = [], []
    for k in range(TOP_K):
        mx = jnp.max(logits, axis=0, keepdims=True)
        sel = jnp.min(jnp.where(logits == mx, eidx, float(N_EXPERTS)), axis=0, keepdims=True)
        hot = eidx == sel
        idx_ref[k:k + 1, :] = sel.astype(I32)
        vals.append(mx)
        hots.append(hot)
        logits = jnp.where(hot, -jnp.inf, logits)
    ex = [jnp.exp(v - vals[0]) for v in vals]
    den = ex[0] + ex[1] + ex[2] + ex[3]
    for k in range(TOP_K):
        gate_ref[k:k + 1, :] = ex[k] / den

    chosen = jnp.zeros((N_EXPERTS, TM_MID), F32)
    for k in range(TOP_K):
        chosen = chosen + jnp.where(hots[k], 1.0, 0.0)
    before = _dot(chosen.astype(BF16), tri_ref[...]) + base_ref[:, 0:1]
    for k in range(TOP_K):
        rank_ref[k:k + 1, :] = jnp.sum(jnp.where(hots[k], before, 0.0), axis=0, keepdims=True).astype(I32)
    base_ref[...] = base_ref[...] + jnp.sum(chosen, axis=1, keepdims=True)
    cnt_ref[...] = base_ref[...]


def _mid(x2d, y_pool, y_diff, w_out, g2, w_cq, kx, vx, w_co, g3, wr_hi, wr_lo, b_r, tri):
    n = N_TOK // TM_MID
    per_batch = SEQ // TM_MID
    tok = lambda i: (i, 0)
    fixed = lambda i: (0, 0)
    kv = pl.BlockSpec((1, MEM_LEN, D_MODEL), lambda i: (i // per_batch, 0, 0))
    row4 = pl.BlockSpec((TOP_K, TM_MID), lambda i: (0, i))
    return pl.pallas_call(
        _mid_kernel,
        grid=(n,),
        in_specs=[
            pl.BlockSpec((TM_MID, D_MODEL), tok),
            pl.BlockSpec((TM_MID, POOL_WIDTH), tok),
            pl.BlockSpec((TM_MID, DIFF_WIDTH), tok),
            pl.BlockSpec((D_MODEL, D_MODEL), fixed),
            pl.BlockSpec((1, D_MODEL), fixed),
            pl.BlockSpec((D_MODEL, D_MODEL), fixed),
            kv, kv,
            pl.BlockSpec((D_MODEL, D_MODEL), fixed),
            pl.BlockSpec((1, D_MODEL), fixed),
            pl.BlockSpec((N_EXPERTS, D_MODEL), fixed),
            pl.BlockSpec((N_EXPERTS, D_MODEL), fixed),
            pl.BlockSpec((N_EXPERTS, 1), fixed),
            pl.BlockSpec((TM_MID, TM_MID), fixed),
        ],
        out_specs=[
            pl.BlockSpec((TM_MID, D_MODEL), tok),
            pl.BlockSpec((TM_MID, D_MODEL), tok),
            row4, row4, row4,
            pl.BlockSpec((N_EXPERTS, LANES), fixed),
        ],
        out_shape=[
            jax.ShapeDtypeStruct((N_TOK, D_MODEL), F32),
            jax.ShapeDtypeStruct((N_TOK, D_MODEL), F32),
            jax.ShapeDtypeStruct((TOP_K, N_TOK), I32),
            jax.ShapeDtypeStruct((TOP_K, N_TOK), F32),
            jax.ShapeDtypeStruct((TOP_K, N_TOK), I32),
            jax.ShapeDtypeStruct((N_EXPERTS, LANES), F32),
        ],
        scratch_shapes=[pltpu.VMEM((TM_MID, D_MODEL), BF16), pltpu.VMEM((N_EXPERTS, LANES), F32)],
        compiler_params=pltpu.CompilerParams(
            dimension_semantics=("arbitrary",), vmem_limit_bytes=56 * 1024 * 1024),
        name="mid_block",
    )(x2d, y_pool, y_diff, w_out, g2, w_cq, kx, vx, w_co, g3, wr_hi, wr_lo, b_r, tri)


def _for_rows(n_rows, fn):
    def body(g, c):
        for j in range(ROW_UNROLL):
            fn(g * ROW_UNROLL + j)
        return c

    lax.fori_loop(0, n_rows // ROW_UNROLL, body, 0)


def _to_row_tiles(dst_ref, val, row0=0):
    n = val.shape[0]
    for s in range(ROW_TILE):
        dst_ref[pl.ds(row0 * ROW_TILE + s, n, stride=ROW_TILE), :] = val[:, s * LANES:(s + 1) * LANES]


def _from_row_tiles(src_ref, n, row0=0):
    return jnp.concatenate(
        [src_ref[pl.ds(row0 * ROW_TILE + s, n, stride=ROW_TILE), :] for s in range(ROW_TILE)], axis=1)


def _tile_rows(row):
    return pl.ds(pl.multiple_of(row * ROW_TILE, ROW_TILE), ROW_TILE)


def _dispatch_kernel(dest_ref, pad_start_ref, pad_len_ref, xn_ref, xs_ref, stage_ref, zero_ref, sems, pad_sem):
    i = pl.program_id(0)
    last = pl.num_programs(0) - 1

    def expert_pad_copies(e, visit):
        row = pad_start_ref[e]
        n = pad_len_ref[e]
        for bit in reversed(range(TM_GMM.bit_length() - 1)):
            size = 1 << bit
            take = (n & size) != 0
            offset = n & ~(2 * size - 1)

            @pl.when(take)
            def _():
                visit(pltpu.make_async_copy(
                    zero_ref.at[pl.ds(0, size * ROW_TILE)],
                    xs_ref.at[pl.ds(pl.multiple_of((row + offset) * ROW_TILE, ROW_TILE), size * ROW_TILE)],
                    pad_sem))

    def unused_block_copies(b, visit):
        @pl.when((b < N_ITEMS) & (b >= pad_len_ref[N_EXPERTS]))
        def _():
            for half in range(2):
                rows = zero_ref.shape[0]
                visit(pltpu.make_async_copy(
                    zero_ref,
                    xs_ref.at[pl.ds(pl.multiple_of(b * TM_GMM * ROW_TILE + half * rows, rows), rows)],
                    pad_sem))

    n_steps = N_TOK // TM_DISP
    blocks_per_step = -(-N_ITEMS // n_steps)

    def start_pad(step):
        @pl.when(step < N_EXPERTS)
        def _():
            expert_pad_copies(step, lambda cp: cp.start())
        for m in range(blocks_per_step):
            unused_block_copies(step + m * n_steps, lambda cp: cp.start())

    def wait_all_pad():
        def per_expert(e, c):
            expert_pad_copies(e, lambda cp: cp.wait())
            return c
        lax.fori_loop(0, N_EXPERTS, per_expert, 0)

        def per_block(b, c):
            unused_block_copies(b, lambda cp: cp.wait())
            return c
        lax.fori_loop(0, N_ITEMS, per_block, 0)

    @pl.when(i == 0)
    def _():
        zero_ref[...] = jnp.zeros_like(zero_ref)

    def copies(step, slot, r):
        return [pltpu.make_async_copy(stage_ref.at[slot, _tile_rows(r)],
                                      xs_ref.at[_tile_rows(dest_ref[k * N_TOK + step * TM_DISP + r])],
                                      sems.at[slot]) for k in range(TOP_K)]

    def start(step, slot):
        def fn(r):
            for k, cp in enumerate(copies(step, slot, r)):
                cp.start(priority=k % 2)
        _for_rows(TM_DISP, fn)

    def wait(step, slot):
        def fn(r):
            for cp in copies(step, slot, r):
                cp.wait()
        _for_rows(TM_DISP, fn)

    for slot in range(2):
        @pl.when(i % 2 == slot)
        def _():
            _to_row_tiles(stage_ref.at[slot], xn_ref[...])
            start(i, slot)

            @pl.when(i > 0)
            def _():
                wait(i - 1, 1 - slot)

            @pl.when(i == last)
            def _():
                wait(i, slot)

    start_pad(i)

    @pl.when(i == last)
    def _():
        wait_all_pad()


def _dispatch(dest_flat, pad_start, pad_len, xn):
    return pl.pallas_call(
        _dispatch_kernel,
        grid_spec=pltpu.PrefetchScalarGridSpec(
            num_scalar_prefetch=3,
            grid=(N_TOK // TM_DISP,),
            in_specs=[pl.BlockSpec((TM_DISP, D_MODEL), lambda i, d, ps, pn: (i, 0))],
            out_specs=pl.BlockSpec(memory_space=pl.ANY),
            scratch_shapes=[pltpu.VMEM((2, TM_DISP * ROW_TILE, LANES), F32),
                            pltpu.VMEM((TM_GMM // 2 * ROW_TILE, LANES), F32),
                            pltpu.SemaphoreType.DMA((2,)), pltpu.SemaphoreType.DMA(())],
        ),
        out_shape=jax.ShapeDtypeStruct((N_PAD_ROWS * ROW_TILE, LANES), F32),
        compiler_params=pltpu.CompilerParams(dimension_semantics=("arbitrary",)),
        name="moe_dispatch",
    )(dest_flat, pad_start, pad_len, xn)


def _gmm_kernel(blk_ref, exp_ref, groups_ref, newe_ref, nexte_ref, xs_ref, wgu_ref, bgu_ref, wd_ref,
                bd_ref, o_ref, wgu_f, wd_f, wgu_s, wd_s, sems):
    w = pl.program_id(0)

    def weight_copies(e):
        return (pltpu.make_async_copy(wgu_ref.at[e], wgu_f, sems.at[0]),
                pltpu.make_async_copy(wd_ref.at[e], wd_f, sems.at[1]))

    @pl.when(w == 0)
    def _():
        for cp in weight_copies(exp_ref[0]):
            cp.start()

    @pl.when(newe_ref[w] == 1)
    def _():
        for cp in weight_copies(exp_ref[w]):
            cp.wait()
        wgu_s[...] = wgu_f[...].astype(BF16)
        wd_s[...] = wd_f[...].astype(BF16)

        @pl.when(nexte_ref[w] < N_EXPERTS)
        def _():
            for cp in weight_copies(nexte_ref[w]):
                cp.start()

    def expert_ffn(n_groups):
        for r0 in range(0, n_groups * GMM_ROWS, GMM_ROWS):
            xb = _from_row_tiles(xs_ref, GMM_ROWS, r0).astype(BF16)
            gate = _dot(xb, wgu_s[:, 0:D_EXPERT]) + bgu_ref[0, :, 0:D_EXPERT]
            up = _dot(xb, wgu_s[:, D_EXPERT:]) + bgu_ref[0, :, D_EXPERT:]
            gate = jnp.minimum(gate, SWIGLU_LIMIT)
            up = jnp.clip(up, -SWIGLU_LIMIT, SWIGLU_LIMIT)
            hid = (up + 1.0) * gate * jax.nn.sigmoid(SWIGLU_ALPHA * gate)
            y = _dot(hid.astype(BF16), wd_s[...]) + bd_ref[0]
            _to_row_tiles(o_ref, y, r0)
        rest = (TM_GMM // GMM_ROWS - n_groups) * GMM_ROWS * ROW_TILE
        if rest:
            o_ref[pl.ds(n_groups * GMM_ROWS * ROW_TILE, rest), :] = jnp.zeros((rest, LANES), F32)

    for n_groups in range(TM_GMM // GMM_ROWS + 1):
        @pl.when(groups_ref[w] == n_groups)
        def _():
            expert_ffn(n_groups)


def _gmm(item_blk, item_exp, item_groups, item_newe, item_nexte, xs, w_gu, b_gu, w_down, b_down):
    rows = lambda w, blk, e, v, ne, nx: (blk[w], 0)
    per_e = lambda w, blk, e, v, ne, nx: (e[w], 0, 0)
    return pl.pallas_call(
        _gmm_kernel,
        grid_spec=pltpu.PrefetchScalarGridSpec(
            num_scalar_prefetch=5,
            grid=(N_ITEMS,),
            in_specs=[
                pl.BlockSpec((TM_GMM * ROW_TILE, LANES), rows),
                pl.BlockSpec(memory_space=pl.ANY),
                pl.BlockSpec((1, 1, 2 * D_EXPERT), per_e),
                pl.BlockSpec(memory_space=pl.ANY),
                pl.BlockSpec((1, 1, D_MODEL), per_e),
            ],
            out_specs=pl.BlockSpec((TM_GMM * ROW_TILE, LANES), lambda w, blk, e, v, ne, nx: (w, 0)),
            scratch_shapes=[pltpu.VMEM((D_MODEL, 2 * D_EXPERT), F32), pltpu.VMEM((D_EXPERT, D_MODEL), F32),
                            pltpu.VMEM((D_MODEL, 2 * D_EXPERT), BF16), pltpu.VMEM((D_EXPERT, D_MODEL), BF16),
                            pltpu.SemaphoreType.DMA((2,))],
        ),
        out_shape=jax.ShapeDtypeStruct((N_PAD_ROWS * ROW_TILE, LANES), F32),
        compiler_params=pltpu.CompilerParams(
            dimension_semantics=("arbitrary",), vmem_limit_bytes=56 * 1024 * 1024),
        name="moe_experts",
    )(item_blk, item_exp, item_groups, item_newe, item_nexte, xs, w_gu, b_gu, w_down, b_down)


def _combine_kernel(dest_ref, ys_ref, x2_ref, gate_ref, g_ref, o_ref, buf0_ref, buf1_ref, sems):
    i = pl.program_id(0)
    last = pl.num_programs(0) - 1
    bufs = (buf0_ref, buf1_ref)

    def copies(step, slot, r):
        return [pltpu.make_async_copy(ys_ref.at[_tile_rows(dest_ref[k * N_TOK + step * TM_COMB + r])],
                                      bufs[slot].at[k, _tile_rows(r)],
                                      sems.at[slot]) for k in range(TOP_K)]

    def start(step, slot):
        def fn(r):
            for k, cp in enumerate(copies(step, slot, r)):
                cp.start(priority=k % 2)
        _for_rows(TM_COMB, fn)

    def wait(step, slot):
        def fn(r):
            for cp in copies(step, slot, r):
                cp.wait()
        _for_rows(TM_COMB, fn)

    @pl.when(i == 0)
    def _():
        start(0, 0)

    nxt = jnp.minimum(i + 1, last)
    for slot in range(2):
        @pl.when(i % 2 == slot)
        def _():
            wait(i, slot)
            for r0 in range(0, TM_COMB, COMB_ROWS):
                for r in range(r0, r0 + COMB_ROWS):
                    for k, cp in enumerate(copies(nxt, 1 - slot, r)):
                        cp.start(priority=k % 2)
                rows = slice(r0, r0 + COMB_ROWS)
                gates = gate_ref[rows, :]
                y = x2_ref[rows, :]
                for k in range(TOP_K):
                    y = y + _from_row_tiles(bufs[slot].at[k], COMB_ROWS, r0) * gates[:, k:k + 1]
                o_ref[rows, :] = _rms(y, g_ref[...])

            @pl.when(i == last)
            def _():
                wait(nxt, 1 - slot)


def _combine(dest_flat, ys, x2, gates_t, g):
    tok = lambda i, d: (i, 0)
    return pl.pallas_call(
        _combine_kernel,
        grid_spec=pltpu.PrefetchScalarGridSpec(
            num_scalar_prefetch=1,
            grid=(N_TOK // TM_COMB,),
            in_specs=[
                pl.BlockSpec(memory_space=pl.ANY),
                pl.BlockSpec((TM_COMB, D_MODEL), tok),
                pl.BlockSpec((TM_COMB, TOP_K), tok),
                pl.BlockSpec((1, D_MODEL), lambda i, d: (0, 0)),
            ],
            out_specs=pl.BlockSpec((TM_COMB, D_MODEL), tok),
            scratch_shapes=[pltpu.VMEM((TOP_K, TM_COMB * ROW_TILE, LANES), F32),
                            pltpu.VMEM((TOP_K, TM_COMB * ROW_TILE, LANES), F32),
                            pltpu.SemaphoreType.DMA((2,))],
        ),
        out_shape=jax.ShapeDtypeStruct((N_TOK, D_MODEL), F32),
        compiler_params=pltpu.CompilerParams(
            dimension_semantics=("arbitrary",), vmem_limit_bytes=48 * 1024 * 1024),
        name="moe_combine",
    )(dest_flat, ys, x2, gates_t, g)


def _rotary_lane_table():
    half = ROT_DIM // 2
    inv_freq = ROPE_THETA ** (-jnp.arange(0, ROT_DIM, 2, dtype=F32) / ROT_DIM)
    rest = jnp.zeros((DIFF_HEAD_DIM - ROT_DIM,), F32)
    zh = jnp.zeros((half,), F32)
    oh = jnp.ones((half,), F32)
    reps = LANES // DIFF_HEAD_DIM
    rows = [jnp.tile(jnp.concatenate(r), reps) for r in
            ([inv_freq, inv_freq, rest], [-oh, zh, rest], [zh, oh, rest])]
    return jnp.concatenate([jnp.stack(rows), jnp.zeros((8 - len(rows), LANES), F32)])


def _work_items(counts):
    n_items = (counts + TM_GMM - 1) // TM_GMM
    item_end = jnp.cumsum(n_items)
    total = item_end[-1]
    starts = (item_end - n_items) * TM_GMM
    w = jnp.arange(N_ITEMS, dtype=I32)
    valid = w < total
    wc = jnp.minimum(w, total - 1)
    e = jnp.minimum(jnp.sum((item_end[None, :] <= wc[:, None]).astype(I32), axis=1), N_EXPERTS - 1)
    is_e = e[:, None] == jnp.arange(N_EXPERTS, dtype=I32)[None, :]
    pick = lambda per_expert: jnp.sum(jnp.where(is_e, per_expert[None, :], 0), axis=1)
    new_e = (valid & (e != jnp.concatenate([jnp.full((1,), -1, I32), e[:-1]]))).astype(I32)
    ids = jnp.arange(N_EXPERTS, dtype=I32)
    later = (ids[None, :] > ids[:, None]) & (counts[None, :] > 0)
    next_of = jnp.min(jnp.where(later, ids[None, :], N_EXPERTS), axis=1).astype(I32)
    next_e = jnp.where(valid, pick(next_of), N_EXPERTS).astype(I32)
    pad_start = (starts + counts).astype(I32)
    pad_len = jnp.concatenate([n_items * TM_GMM - counts, total[None]]).astype(I32)
    real_rows = jnp.clip(pick(counts) - (w - pick(item_end - n_items)) * TM_GMM, 0, TM_GMM)
    groups = jnp.where(valid, (real_rows + GMM_ROWS - 1) // GMM_ROWS, 0).astype(I32)
    return wc.astype(I32), e.astype(I32), groups, new_e, next_e, starts.astype(I32), pad_start, pad_len


def kernel(x, positions, mem, attn_norm_g, w_in, w_pool, pool_scale, lambda_q1, lambda_k1, lambda_q2, lambda_k2, subln_g, w_out, xattn_norm_g, mem_norm_g, w_cq, w_ckv, w_co, ffn_norm_g, w_router, b_router, w_gu, b_gu, w_down, b_down, final_norm_g):
    l = 0
    x2d = x.reshape(N_TOK, D_MODEL)
    u_pool, q, k, v = _in_proj(x2d, attn_norm_g[l].reshape(1, D_MODEL), w_in[l].astype(BF16),
                               positions.reshape(N_TOK, 1), _rotary_lane_table())
    y_pool = _pool(u_pool.reshape(BATCH, SEQ, POOL_WIDTH), w_pool[l].astype(BF16),
                   pool_scale[l].reshape(1, POOL_WIDTH))
    lam_vecs = jnp.stack([lambda_q1[l], lambda_k1[l], lambda_q2[l], lambda_k2[l]]).astype(F32)
    y_diff = _diff_attn(lam_vecs, q.reshape(BATCH, SEQ, QK_WIDTH), k.reshape(BATCH, SEQ, QK_WIDTH),
                        v.reshape(BATCH, SEQ, DIFF_WIDTH), subln_g[l].reshape(1, DIFF_V_DIM))
    kx, vx = _mem_kv(mem, mem_norm_g[l].reshape(1, D_MODEL), w_ckv[l].astype(BF16))

    wr_t = w_router[l].T
    wr_hi = wr_t.astype(BF16)
    wr_lo = (wr_t - wr_hi.astype(F32)).astype(BF16)
    tri = (jnp.arange(TM_MID)[:, None] < jnp.arange(TM_MID)[None, :]).astype(BF16)
    x2, xn, top_idx, gates, rank, cnt = _mid(
        x2d, y_pool.reshape(N_TOK, POOL_WIDTH), y_diff.reshape(N_TOK, DIFF_WIDTH), w_out[l].astype(BF16),
        xattn_norm_g[l].reshape(1, D_MODEL), w_cq[l].astype(BF16), kx, vx, w_co[l].astype(BF16),
        ffn_norm_g[l].reshape(1, D_MODEL), wr_hi, wr_lo, b_router[l].reshape(N_EXPERTS, 1), tri)

    counts = cnt[:, 0].astype(I32)
    item_blk, item_exp, item_groups, item_newe, item_nexte, starts, pad_start, pad_len = _work_items(counts)
    hot = top_idx[:, :, None] == jnp.arange(N_EXPERTS, dtype=I32)
    dest = jnp.sum(jnp.where(hot, starts, 0), axis=-1) + rank
    dest_flat = dest.reshape(N_ROWS)

    xs = _dispatch(dest_flat, pad_start, pad_len, xn)
    ys = _gmm(item_blk, item_exp, item_groups, item_newe, item_nexte, xs,
              w_gu[l], b_gu[l].reshape(N_EXPERTS, 1, 2 * D_EXPERT),
              w_down[l], b_down[l].reshape(N_EXPERTS, 1, D_MODEL))
    out = _combine(dest_flat, ys, x2, gates.T, final_norm_g.reshape(1, D_MODEL))
    return out.reshape(BATCH, SEQ, D_MODEL)
```
